```python
import math
import jax, jax.numpy as jnp
from jax import lax
import numpy as np

D_MODEL = 1024
BATCH = 32
SEQ = 2048
DEPTH = 4

CTX_LEN = 256
GRID_W = 64
N_MIXERS = 3
ROPE_BASE = 10000.0
EPS = 1e-6
BLOCK = 128

A_HEADS = 16
A_KV_HEADS = 4
A_GROUP = A_HEADS // A_KV_HEADS
A_HEAD_DIM = D_MODEL // A_HEADS
A_WIDTH = A_HEADS * A_HEAD_DIM
A_KV_WIDTH = A_KV_HEADS * A_HEAD_DIM
A_IN = 2 * A_WIDTH + 2 * A_KV_WIDTH
WINDOW = 128

B_HEADS = 16
B_NOPE = 64
B_ROPE = 32
B_V = 64
B_Q_LORA = D_MODEL // 2
B_KV_LORA = D_MODEL // 4
B_WIDTH = B_HEADS * B_V
B_IN = B_Q_LORA + B_KV_LORA + B_ROPE + B_WIDTH

C_INNER = 2 * D_MODEL
C_HEAD_DIM = 64
C_HEADS = C_INNER // C_HEAD_DIM
C_GROUPS = 4
C_HPG = C_HEADS // C_GROUPS
C_STATE = 128
C_CONV_K = 3
C_CHUNK = 128
C_CONV_DIM = C_INNER + 2 * C_GROUPS * C_STATE
C_IN = C_INNER + C_CONV_DIM + 2 * C_HEADS
DT_MIN = 1e-3
DT_MAX = 1e-1

kernel_name = "hybrid_interleaved_swa_mla_ssd_prefix_ctx"

F32 = jnp.float32


def rms_norm(u, g):
    uf = u.astype(F32)
    y = uf * lax.rsqrt(jnp.mean(uf * uf, axis=-1, keepdims=True) + EPS)
    return (y * g.astype(F32)).astype(u.dtype)


def adaln(cond, w, bias):
    m = jax.nn.silu(cond) @ w + bias
    return jnp.split(m, 3, axis=-1)


def axial_rope(rows, dim, dtype):
    row = jnp.repeat(jnp.arange(rows), GRID_W).astype(F32)
    col = (jnp.arange(rows * GRID_W) % GRID_W).astype(F32)
    nf = dim // 4
    inv = ROPE_BASE ** (-jnp.arange(nf, dtype=F32) / nf)
    ar = row[:, None] * inv
    ac = col[:, None] * inv
    ang = jnp.concatenate([ar, ar, ac, ac], axis=-1)
    return jnp.cos(ang).astype(dtype), jnp.sin(ang).astype(dtype)


def apply_rope(u, cos, sin):
    d = u.shape[-1]
    shape = (cos.shape[0],) + (1,) * (u.ndim - 3) + (d,)
    r1, r2, c1, c2 = jnp.split(u, 4, axis=-1)
    rot = jnp.concatenate([-r2, r1, -c2, c1], axis=-1)
    return u * cos.reshape(shape) + rot * sin.reshape(shape)


def softmax_with_sink(s, sink):
    sk = jnp.broadcast_to(sink, s.shape[:-1] + (1,))
    p = jax.nn.softmax(jnp.concatenate([s, sk], axis=-1), axis=-1)
    return p[..., :-1]


def band_blocks(u, nb):
    pad = [(0, 0), (BLOCK, BLOCK)] + [(0, 0)] * (u.ndim - 2)
    up = jnp.pad(u, pad).reshape((u.shape[0], nb + 2, BLOCK) + u.shape[2:])
    return jnp.concatenate([up[:, :-2], up[:, 1:-1], up[:, 2:]], axis=2)


def window_gqa(hx, hc, w_in, sink, w_out, cos, sin, need_ctx):
    b, T, _ = hx.shape
    scale = A_HEAD_DIM ** -0.5
    sink_b = sink.astype(F32).reshape(A_KV_HEADS, A_GROUP)[None, :, :, None, None]

    def project(h):
        n = h.shape[1]
        q, k, v, g = jnp.split(h @ w_in, [A_WIDTH, A_WIDTH + A_KV_WIDTH, A_WIDTH + 2 * A_KV_WIDTH], axis=-1)
        return (q.reshape(b, n, A_KV_HEADS, A_GROUP, A_HEAD_DIM),
                k.reshape(b, n, A_KV_HEADS, A_HEAD_DIM),
                v.reshape(b, n, A_KV_HEADS, A_HEAD_DIM), g)

    qx, kx, vx, gx = project(hx)
    qc, kc, vc, gc = project(hc)
    qx = apply_rope(qx, cos, sin)
    kx = apply_rope(kx, cos, sin)

    nb = T // BLOCK
    qi = jnp.arange(BLOCK)[:, None]
    kj = jnp.arange(3 * BLOCK)[None, :]
    near = jnp.abs(kj - BLOCK - qi) <= WINDOW
    kpos = jnp.arange(nb)[:, None] * BLOCK - BLOCK + jnp.arange(3 * BLOCK)[None, :]
    inside = (kpos >= 0) & (kpos < T)
    mask = near[None] & inside[:, None, :]

    qb = jnp.moveaxis(qx.reshape(b, nb, BLOCK, A_KV_HEADS, A_GROUP, A_HEAD_DIM), 1, 0)
    kw = jnp.moveaxis(band_blocks(kx, nb), 1, 0)
    vw = jnp.moveaxis(band_blocks(vx, nb), 1, 0)

    def attend_block(args):
        q, k, v, m = args
        s_loc = jnp.einsum("bqhgd,bkhd->bhgqk", q, k).astype(F32) * scale
        s_loc = jnp.where(m, s_loc, -jnp.inf)
        s_ctx = jnp.einsum("bqhgd,bchd->bhgqc", q, kc).astype(F32) * scale
        p = softmax_with_sink(jnp.concatenate([s_loc, s_ctx], axis=-1), sink_b).astype(v.dtype)
        return (jnp.einsum("bhgqk,bkhd->bqhgd", p[..., :3 * BLOCK], v)
                + jnp.einsum("bhgqc,bchd->bqhgd", p[..., 3 * BLOCK:], vc))

    ox = lax.map(attend_block, (qb, kw, vw, mask))
    ox = jnp.moveaxis(ox, 0, 1).reshape(b, T, A_WIDTH)
    yx = (ox * jax.nn.silu(gx)) @ w_out
    yc = None
    if need_ctx:
        sc = jnp.einsum("bqhgd,bkhd->bhgqk", qc, kc).astype(F32) * scale
        pc = softmax_with_sink(sc, sink_b).astype(vc.dtype)
        oc = jnp.einsum("bhgqk,bkhd->bqhgd", pc, vc).reshape(b, hc.shape[1], A_WIDTH)
        yc = (oc * jax.nn.silu(gc)) @ w_out
    return yx, yc


def latent_attention(hx, hc, w_in, q_norm, w_uq, kv_norm, w_ukv, w_out, cos, sin, need_ctx):
    b, T, _ = hx.shape
    scale = (B_NOPE + B_ROPE) ** -0.5

    def project(h):
        n = h.shape[1]
        cq, ckv, kr, g = jnp.split(h @ w_in, [B_Q_LORA, B_Q_LORA + B_KV_LORA, B_Q_LORA + B_KV_LORA + B_ROPE], axis=-1)
        q = (rms_norm(cq, q_norm) @ w_uq).reshape(b, n, B_HEADS, B_NOPE + B_ROPE)
        kv = (rms_norm(ckv, kv_norm) @ w_ukv).reshape(b, n, B_HEADS, B_NOPE + B_V)
        return q[..., :B_NOPE], q[..., B_NOPE:], kv[..., :B_NOPE], kr, kv[..., B_NOPE:], g

    qnx, qrx, knx, krx, vx, gx = project(hx)
    qnc, qrc, knc, krc, vc, gc = project(hc)
    qrx = apply_rope(qrx, cos, sin)
    krx = apply_rope(krx, cos, sin)

    def attend(qn, qr, kn, kr, v):
        s = (jnp.einsum("bqhd,bkhd->bhqk", qn, kn)
             + jnp.einsum("bqhd,bkd->bhqk", qr, kr)).astype(F32) * scale
        p = jax.nn.softmax(s, axis=-1).astype(v.dtype)
        return jnp.einsum("bhqk,bkhd->bqhd", p, v)

    kn_all = jnp.concatenate([knc, knx], axis=1)
    kr_all = jnp.concatenate([krc, krx], axis=1)
    v_all = jnp.concatenate([vc, vx], axis=1)
    nb = T // BLOCK
    to_blocks = lambda u: jnp.moveaxis(u.reshape((b, nb, BLOCK) + u.shape[2:]), 1, 0)
    ox = lax.map(lambda qq: attend(qq[0], qq[1], kn_all, kr_all, v_all), (to_blocks(qnx), to_blocks(qrx)))
    ox = jnp.moveaxis(ox, 0, 1).reshape(b, T, B_WIDTH)
    yx = (ox * jax.nn.silu(gx)) @ w_out
    yc = None
    if need_ctx:
        oc = attend(qnc, qrc, knc, krc, vc).reshape(b, hc.shape[1], B_WIDTH)
        yc = (oc * jax.nn.silu(gc)) @ w_out
    return yx, yc


def depthwise_conv(u, w, bias):
    pad = C_CONV_K // 2
    out = lax.conv_general_dilated(u, w[:, None, :].astype(u.dtype), (1,), [(pad, pad)],
                                   dimension_numbers=("NWC", "WIO", "NWC"),
                                   feature_group_count=u.shape[-1])
    return out + bias


def segsum(a):
    cs = jnp.cumsum(a, axis=-1)
    diff = cs[..., :, None] - cs[..., None, :]
    L = a.shape[-1]
    return jnp.where(jnp.tril(jnp.ones((L, L), bool)), diff, -jnp.inf)


def ssd_chunked(X, dtA, Bm, Cm, init):
    b, n, G, R, P = X.shape
    nc = n // C_CHUNK
    Xc = X.astype(F32).reshape(b, nc, C_CHUNK, G, R, P)
    Bc = Bm.astype(F32).reshape(b, nc, C_CHUNK, G, C_STATE)
    Cc = Cm.astype(F32).reshape(b, nc, C_CHUNK, G, C_STATE)
    Ac = dtA.astype(F32).reshape(b, nc, C_CHUNK, G, R).transpose(0, 3, 4, 1, 2)
    A_cum = jnp.cumsum(Ac, axis=-1)
    Lmat = jnp.exp(segsum(Ac))
    CB = jnp.einsum("bclgn,bcsgn->bgcls", Cc, Bc)
    y_diag = jnp.einsum("bgcls,bgrcls,bcsgrp->bclgrp", CB, Lmat, Xc)
    decay_states = jnp.exp(A_cum[..., -1:] - A_cum)
    states = jnp.einsum("bclgn,bgrcl,bclgrp->bcgrpn", Bc, decay_states, Xc)
    states = jnp.concatenate([init[:, None], states], axis=1)
    chunk_decay = jnp.exp(segsum(jnp.pad(A_cum[..., -1], ((0, 0), (0, 0), (0, 0), (1, 0)))))
    new_states = jnp.einsum("bgrzc,bcgrpn->bzgrpn", chunk_decay, states)
    y_off = jnp.einsum("bclgn,bcgrpn,bgrcl->bclgrp", Cc, new_states[:, :-1], jnp.exp(A_cum))
    return (y_diag + y_off).reshape(b, n, G, R, P), new_states[:, -1]


def bidirectional_ssd(hx, hc, w_in, conv_w, conv_b, dt_bias, a_log, d_skip, norm_w, w_out, need_ctx):
    b = hx.shape[0]
    A = -jnp.exp(a_log.astype(F32))

    def prep(h):
        n = h.shape[1]
        z, xbc, dt = jnp.split(h @ w_in, [C_INNER, C_INNER + C_CONV_DIM], axis=-1)
        xbc = jax.nn.silu(depthwise_conv(xbc, conv_w, conv_b))
        xs, Bm, Cm = jnp.split(xbc, [C_INNER, C_INNER + C_GROUPS * C_STATE], axis=-1)
        dt = jax.nn.softplus(dt.astype(F32).reshape(b, n, 2, C_HEADS) + dt_bias.astype(F32))
        return (z, xs.reshape(b, n, C_GROUPS, C_HPG, C_HEAD_DIM),
                Bm.reshape(b, n, C_GROUPS, C_STATE), Cm.reshape(b, n, C_GROUPS, C_STATE), dt)

    def scan_dir(d, xs, Bm, Cm, dt, init):
        dtd = dt[:, :, d].reshape(b, xs.shape[1], C_GROUPS, C_HPG)
        return ssd_chunked(xs.astype(F32) * dtd[..., None], dtd * A[d].reshape(C_GROUPS, C_HPG), Bm, Cm, init)

    flip = lambda u: jnp.flip(u, axis=1)
    zx, xx, Bx, Cx, dtx = prep(hx)
    zc, xc, Bc, Cc, dtc = prep(hc)
    init = jnp.zeros((b, C_GROUPS, C_HPG, C_HEAD_DIM, C_STATE), F32)
    yf_c, s_f = scan_dir(0, xc, Bc, Cc, dtc, init)
    yf_x, _ = scan_dir(0, xx, Bx, Cx, dtx, s_f)
    yb_c, s_b = scan_dir(1, flip(xc), flip(Bc), flip(Cc), flip(dtc), init)
    yb_x, _ = scan_dir(1, flip(xx), flip(Bx), flip(Cx), flip(dtx), s_b)
    dsk = d_skip.astype(F32).reshape(C_GROUPS, C_HPG, 1)

    def finish(yf, yb, xs, z):
        n = xs.shape[1]
        y = (yf + flip(yb) + dsk * xs.astype(F32)).reshape(b, n, C_INNER)
        y = y * jax.nn.silu(z.astype(F32))
        yg = y.reshape(b, n, C_GROUPS, C_INNER // C_GROUPS)
        yg = yg * lax.rsqrt(jnp.mean(yg * yg, axis=-1, keepdims=True) + EPS)
        y = (yg.reshape(b, n, C_INNER) * norm_w.astype(F32)).astype(xs.dtype)
        return y @ w_out

    yx = finish(yf_x, yb_x, xx, zx)
    yc = finish(yf_c, yb_c, xc, zc) if need_ctx else None
    return yx, yc


def setup_inputs(seed: int = 0) -> dict:
    key = jax.random.key(seed)
    ks = iter(jax.random.split(key, 40))

    def nrm(shape, scale):
        return jax.random.normal(next(ks), shape, jnp.float32) * scale

    def gain(shape):
        return 1.0 + nrm(shape, 0.05)

    D = D_MODEL
    n_a, n_b, n_c = (len(range(k, DEPTH, N_MIXERS)) for k in range(N_MIXERS))
    dt_cols = jnp.where(jnp.arange(C_IN) >= C_INNER + C_CONV_DIM, 0.1, 1.0).astype(jnp.float32)
    dt0 = jnp.exp(jax.random.uniform(next(ks), (n_c, 2, C_HEADS), jnp.float32, math.log(DT_MIN), math.log(DT_MAX)))
    a_init = jax.random.uniform(next(ks), (n_c, 2, C_HEADS), jnp.float32, 1.0, 16.0)
    return {
        "x": nrm((BATCH, SEQ, D), 1.0),
        "c": nrm((BATCH, D), 1.0),
        "ctx": nrm((BATCH, CTX_LEN, D), 1.0),
        "c_ctx": nrm((D,), 1.0),
        "ada_w": nrm((DEPTH, D, 3 * D), 0.5 * D ** -0.5),
        "ada_b": nrm((DEPTH, 3 * D), 0.02),
        "norm_g": gain((DEPTH, D)),
        "final_g": gain((D,)),
        "a_w_in": nrm((n_a, D, A_IN), D ** -0.5),
        "a_sink": nrm((n_a, A_HEADS), 1.0),
        "a_w_out": nrm((n_a, A_WIDTH, D), A_WIDTH ** -0.5),
        "b_w_in": nrm((n_b, D, B_IN), D ** -0.5),
        "b_q_norm": gain((n_b, B_Q_LORA)),
        "b_w_uq": nrm((n_b, B_Q_LORA, B_HEADS * (B_NOPE + B_ROPE)), B_Q_LORA ** -0.5),
        "b_kv_norm": gain((n_b, B_KV_LORA)),
        "b_w_ukv": nrm((n_b, B_KV_LORA, B_HEADS * (B_NOPE + B_V)), B_KV_LORA ** -0.5),
        "b_w_out": nrm((n_b, B_WIDTH, D), B_WIDTH ** -0.5),
        "c_w_in": nrm((n_c, D, C_IN), D ** -0.5) * dt_cols,
        "c_conv_w": nrm((n_c, C_CONV_K, C_CONV_DIM), C_CONV_K ** -0.5),
        "c_conv_b": nrm((n_c, C_CONV_DIM), 0.02),
        "c_dt_bias": dt0 + jnp.log(-jnp.expm1(-dt0)),
        "c_a_log": jnp.log(a_init),
        "c_d": 1.0 + nrm((n_c, C_HEADS), 0.1),
        "c_norm": gain((n_c, C_INNER)),
        "c_w_out": nrm((n_c, C_INNER, D), C_INNER ** -0.5),
    }


def reference(x, c, ctx, c_ctx, ada_w, ada_b, norm_g, final_g,
              a_w_in, a_sink, a_w_out,
              b_w_in, b_q_norm, b_w_uq, b_kv_norm, b_w_ukv, b_w_out,
              c_w_in, c_conv_w, c_conv_b, c_dt_bias, c_a_log, c_d, c_norm, c_w_out):
    T = x.shape[1]
    ROWS = T // GRID_W
    cos_a, sin_a = axial_rope(ROWS, A_HEAD_DIM, x.dtype)
    cos_b, sin_b = axial_rope(ROWS, B_ROPE, x.dtype)
    for i in range(DEPTH):
        kind = i % N_MIXERS
        j = i // N_MIXERS
        need_ctx = i < DEPTH - 1
        shift_x, scale_x, gate_x = adaln(c, ada_w[i], ada_b[i])
        shift_c, scale_c, gate_c = adaln(c_ctx, ada_w[i], ada_b[i])
        hx = rms_norm(x, norm_g[i]) * (1.0 + scale_x[:, None]) + shift_x[:, None]
        hc = rms_norm(ctx, norm_g[i]) * (1.0 + scale_c) + shift_c
        if kind == 0:
            yx, yc = window_gqa(hx, hc, a_w_in[j], a_sink[j], a_w_out[j], cos_a, sin_a, need_ctx)
        elif kind == 1:
            yx, yc = latent_attention(hx, hc, b_w_in[j], b_q_norm[j], b_w_uq[j], b_kv_norm[j],
                                      b_w_ukv[j], b_w_out[j], cos_b, sin_b, need_ctx)
        else:
            yx, yc = bidirectional_ssd(hx, hc, c_w_in[j], c_conv_w[j], c_conv_b[j], c_dt_bias[j],
                                       c_a_log[j], c_d[j], c_norm[j], c_w_out[j], need_ctx)
        x = x + gate_x[:, None] * yx
        if need_ctx:
            ctx = ctx + gate_c * yc
    return rms_norm(x, final_g)
```

```python
import functools
import math

import jax
import jax.numpy as jnp
from jax import lax
from jax.experimental import pallas as pl
from jax.experimental.pallas import tpu as pltpu

F32 = jnp.float32
BF16 = jnp.bfloat16

GRID_W = 64
N_MIXERS = 3
ROPE_BASE = 10000.0
EPS = 1e-6
BLOCK = 128
WINDOW = 128

A_HEADS = 16
A_KV_HEADS = 4
A_GROUP = A_HEADS // A_KV_HEADS
A_HEAD_DIM = 64

B_HEADS = 16
B_NOPE = 64
B_ROPE = 32
B_V = 64

C_HEAD_DIM = 64
C_GROUPS = 4
C_STATE = 128
C_CHUNK = 128

LANES = 128
ROW_TILE = 256
VMEM_LIMIT = 56 * 1024 * 1024
NEG = -1e30


def _cparams(sem):
    return pltpu.CompilerParams(dimension_semantics=sem, vmem_limit_bytes=VMEM_LIMIT)


def _dot(a, b):
    return jnp.dot(a, b, preferred_element_type=F32)


def _dot_nt(a, b):
    return lax.dot_general(a, b, (((1,), (1,)), ((), ())), preferred_element_type=F32)


def _silu(v):
    return v / (1.0 + jnp.exp(-v))


def _split2(a):
    hi = a.astype(BF16)
    lo = (a - hi.astype(F32)).astype(BF16)
    return hi, lo


def _dot_f32(a, b):
    a_hi, a_lo = _split2(a)
    b_hi, b_lo = _split2(b)
    return _dot(a_hi, b_hi) + _dot(a_hi, b_lo) + _dot(a_lo, b_hi)


def _norm_mod(x, g, mod):
    ms = jnp.mean(x * x, axis=-1, keepdims=True)
    y = x * lax.rsqrt(ms + EPS) * g
    return y * (1.0 + mod[1:2]) + mod[0:1]


def _adaln_body(cond_ref, w_ref, b_ref, o_ref):
    o_ref[0] = _dot_f32(_silu(cond_ref[...]), w_ref[0]) + b_ref[0]


def _adaln_all(cond, ada_w, ada_b):
    depth, d, _ = ada_w.shape
    r = cond.shape[0]
    return pl.pallas_call(
        _adaln_body,
        out_shape=jax.ShapeDtypeStruct((depth, r, 3 * d), F32),
        grid=(depth, 3),
        in_specs=[pl.BlockSpec((r, d), lambda i, j: (0, 0)),
                  pl.BlockSpec((1, d, d), lambda i, j: (i, 0, j)),
                  pl.BlockSpec((1, 1, d), lambda i, j: (i, 0, j))],
        out_specs=pl.BlockSpec((1, r, d), lambda i, j: (i, 0, j)),
        compiler_params=_cparams(("arbitrary", "arbitrary")),
        name="adaln_mod",
    )(cond, ada_w, ada_b.reshape(depth, 1, 3 * d))


def _rope_tables(rows, dim, ctx_len, lane_of_dim, scale):
    row = jnp.repeat(jnp.arange(rows), GRID_W).astype(F32)
    col = (jnp.arange(rows * GRID_W) % GRID_W).astype(F32)
    nf = dim // 4
    inv = ROPE_BASE ** (-jnp.arange(nf, dtype=F32) / nf)
    ar = row[:, None] * inv
    ac = col[:, None] * inv
    ang = jnp.concatenate([ar, ar, ac, ac], axis=-1)
    cos, sin = jnp.cos(ang), jnp.sin(ang)
    sign = jnp.where((jnp.arange(dim) // nf) % 2 == 0, -1.0, 1.0).astype(F32)
    sin = sin * sign
    idx = jnp.asarray(lane_of_dim)
    valid = (idx >= 0)[None, :]
    cos_l = jnp.where(valid, cos[:, jnp.maximum(idx, 0)], 1.0)
    sin_l = jnp.where(valid, sin[:, jnp.maximum(idx, 0)], 0.0)
    cos_l = jnp.concatenate([jnp.ones((ctx_len, LANES), F32), cos_l], axis=0)
    sin_l = jnp.concatenate([jnp.zeros((ctx_len, LANES), F32), sin_l], axis=0)
    return cos_l * scale, sin_l * scale


def _rope128(u, cos, sin, quarter, even):
    fwd = pltpu.roll(u, quarter, 1)
    bwd = pltpu.roll(u, LANES - quarter, 1)
    return u * cos + jnp.where(even, bwd, fwd) * sin


def _proj_a_body(x_ref, mod_ref, g_ref, w_ref, cq_ref, sq_ref, ck_ref, sk_ref,
                 q_ref, k_ref, v_ref, sg_ref):
    tm = x_ref.shape[1]
    h = _norm_mod(x_ref[0], g_ref[...], mod_ref[0]).astype(BF16)
    y = _dot(h, w_ref[...])
    lane = lax.broadcasted_iota(jnp.int32, (tm, LANES), 1)
    even = ((lane % A_HEAD_DIM) // (A_HEAD_DIM // 4)) % 2 == 0
    low = lane < A_HEAD_DIM
    qw = A_HEADS * A_HEAD_DIM
    kw = A_KV_HEADS * A_HEAD_DIM
    cq, sq, ck, sk = cq_ref[...], sq_ref[...], ck_ref[...], sk_ref[...]
    for j in range(qw // LANES):
        u = y[:, j * LANES:(j + 1) * LANES]
        q_ref[0, :, j * LANES:(j + 1) * LANES] = _rope128(u, cq, sq, A_HEAD_DIM // 4, even).astype(BF16)

    def pad_pair(grp, dst_ref):
        swp = pltpu.roll(grp, A_HEAD_DIM, 1)
        zero = jnp.zeros_like(grp)
        return (jnp.where(low, grp, zero), jnp.where(low, zero, swp),
                jnp.where(low, swp, zero), jnp.where(low, zero, grp))

    for j in range(kw // LANES):
        kg = _rope128(y[:, qw + j * LANES: qw + (j + 1) * LANES], ck, sk, A_HEAD_DIM // 4, even)
        vg = y[:, qw + kw + j * LANES: qw + kw + (j + 1) * LANES]
        for dst, grp in ((k_ref, kg), (v_ref, vg)):
            forms = pad_pair(grp, dst)
            for n, f in enumerate(forms):
                c0 = (4 * j + n) * LANES
                dst[0, :, c0:c0 + LANES] = f.astype(BF16)
    g = y[:, qw + 2 * kw:]
    sg_ref[0] = _silu(g).astype(BF16)


def _proj_a(xc, mod, g, w_bf, tabs, n_ctx_tiles, bsz):
    b, s, d = xc.shape
    tm = ROW_TILE
    qw = A_HEADS * A_HEAD_DIM
    row = lambda bb, i: (bb, i, 0)
    modrow = lambda bb, i: (jnp.where(i < n_ctx_tiles, bsz, bb), 0, 0)
    const2 = lambda bb, i: (0, 0)
    tab = pl.BlockSpec((tm, LANES), lambda bb, i: (i, 0))
    out = jax.ShapeDtypeStruct((b, s, qw), BF16)
    return pl.pallas_call(
        _proj_a_body,
        out_shape=(out, out, out, out),
        grid=(b, s // tm),
        in_specs=[pl.BlockSpec((1, tm, d), row),
                  pl.BlockSpec((1, 3, d), modrow),
                  pl.BlockSpec((1, d), const2),
                  pl.BlockSpec(w_bf.shape, const2),
                  tab, tab, tab, tab],
        out_specs=tuple(pl.BlockSpec((1, tm, qw), row) for _ in range(4)),
        compiler_params=_cparams(("parallel", "arbitrary")),
        name="proj_a",
    )(xc, mod, g, w_bf, *tabs)


def _attn_a_body(sink_ref, q_ref, kp_ref, ko_ref, kn_ref, kc_ref, vp_ref, vo_ref, vn_ref, vc_ref,
                 sg_ref, o_ref, kl_ref, vl_ref, *, first_blk, ctx_blocks, lat_blocks):
    blk = pl.program_id(1) + first_blk
    jl = blk - ctx_blocks
    nq = BLOCK
    rows2 = 2 * nq
    row = lax.broadcasted_iota(jnp.int32, (rows2, 1), 0)

    def heads(local):
        if local:
            kl_ref[0:nq] = kp_ref[0]
            kl_ref[nq:2 * nq] = ko_ref[0]
            kl_ref[2 * nq:3 * nq] = kn_ref[0]
            vl_ref[0:nq] = vp_ref[0]
            vl_ref[nq:2 * nq] = vo_ref[0]
            vl_ref[2 * nq:3 * nq] = vn_ref[0]
            qi = lax.broadcasted_iota(jnp.int32, (rows2, 3 * nq), 0) % nq
            col = lax.broadcasted_iota(jnp.int32, (rows2, 3 * nq), 1)
            kk = col % nq
            piece = col // nq
            far = 4 * nq
            no_prev = jnp.where(jl > 0, 0, far)
            no_next = jnp.where(jl < lat_blocks - 1, 0, far)
            ok = ((piece == 1)
                  | ((piece == 0) & (kk - qi >= no_prev))
                  | ((piece == 2) & (qi - kk >= no_next)))
        for h in range(A_KV_HEADS):
            c0 = h * A_GROUP * A_HEAD_DIM
            lhs = jnp.concatenate([q_ref[0, :, c0:c0 + LANES], q_ref[0, :, c0 + LANES:c0 + 2 * LANES]],
                                  axis=0)
            acc = None
            for half in range(2):
                kc0 = (2 * h + half) * LANES
                snk = jnp.where(row < nq, sink_ref[h * A_GROUP + half], sink_ref[h * A_GROUP + 2 + half])
                s_ctx = _dot_nt(lhs, kc_ref[0, :, kc0:kc0 + LANES])
                m = jnp.maximum(jnp.max(s_ctx, axis=-1, keepdims=True), snk)
                if local:
                    s_loc = _dot_nt(lhs, kl_ref[:, kc0:kc0 + LANES])
                    s_loc = jnp.where(ok, s_loc, NEG)
                    m = jnp.maximum(m, jnp.max(s_loc, axis=-1, keepdims=True))
                p_ctx = jnp.exp(s_ctx - m)
                l = jnp.sum(p_ctx, axis=-1, keepdims=True) + jnp.exp(snk - m)
                pv = _dot(p_ctx.astype(BF16), vc_ref[0, :, kc0:kc0 + LANES])
                if local:
                    p_loc = jnp.exp(s_loc - m)
                    l = l + jnp.sum(p_loc, axis=-1, keepdims=True)
                    pv = pv + _dot(p_loc.astype(BF16), vl_ref[:, kc0:kc0 + LANES])
                pv = pv * (1.0 / l)
                acc = pv if acc is None else acc + pv
            o_ref[0, :, c0:c0 + LANES] = (acc[:nq] * sg_ref[0, :, c0:c0 + LANES].astype(F32)).astype(BF16)
            o_ref[0, :, c0 + LANES:c0 + 2 * LANES] = (
                acc[nq:] * sg_ref[0, :, c0 + LANES:c0 + 2 * LANES].astype(F32)).astype(BF16)

    if first_blk >= ctx_blocks:
        heads(True)
    else:
        pl.when(jl >= 0)(lambda: heads(True))
        pl.when(jl < 0)(lambda: heads(False))


def _attn_a(q, kpad, vpad, sg, sink, ctx_len, with_ctx_queries):
    b, s, qw = q.shape
    ctx_blocks = ctx_len // BLOCK
    lat_blocks = (s - ctx_len) // BLOCK
    first_blk = 0 if with_ctx_queries else ctx_blocks
    nblk = s // BLOCK - first_blk

    def lat(i):
        return jnp.maximum(i + first_blk - ctx_blocks, 0)

    own = lambda bb, i: (bb, i + first_blk, 0)
    prev = lambda bb, i: (bb, ctx_blocks + jnp.maximum(lat(i) - 1, 0), 0)
    nxt = lambda bb, i: (bb, ctx_blocks + jnp.minimum(lat(i) + 1, lat_blocks - 1), 0)
    ctxm = lambda bb, i: (bb, 0, 0)
    blk = lambda m: pl.BlockSpec((1, BLOCK, qw), m)
    cblk = pl.BlockSpec((1, ctx_len, qw), ctxm)
    body = functools.partial(_attn_a_body, first_blk=first_blk, ctx_blocks=ctx_blocks,
                             lat_blocks=lat_blocks)
    return pl.pallas_call(
        body,
        out_shape=jax.ShapeDtypeStruct((b, s, qw), BF16),
        grid=(b, nblk),
        in_specs=[pl.BlockSpec(memory_space=pltpu.SMEM),
                  blk(own), blk(prev), blk(own), blk(nxt), cblk,
                  blk(prev), blk(own), blk(nxt), cblk,
                  blk(own)],
        out_specs=blk(own),
        scratch_shapes=[pltpu.VMEM((3 * BLOCK, qw), BF16), pltpu.VMEM((3 * BLOCK, qw), BF16)],
        compiler_params=_cparams(("parallel", "arbitrary")),
        name="attn_a",
    )(sink, q, kpad, kpad, kpad, kpad, vpad, vpad, vpad, vpad, sg)


def _out_body(x_ref, a_ref, w_ref, mod_ref, o_ref):
    y = _dot(a_ref[0], w_ref[...])
    o_ref[0] = x_ref[0] + mod_ref[0][2:3] * y


def _out_final_body(x_ref, a_ref, w_ref, mod_ref, fg_ref, o_ref):
    y = _dot(a_ref[0], w_ref[...])
    xn = x_ref[0] + mod_ref[0][2:3] * y
    ms = jnp.mean(xn * xn, axis=-1, keepdims=True)
    o_ref[0] = xn * lax.rsqrt(ms + EPS) * fg_ref[...]


def _out_proj(xc, a, w_bf, mod, n_ctx_tiles, bsz, final_g=None):
    b, s, d = xc.shape
    tm = ROW_TILE
    width = a.shape[-1]
    const2 = lambda bb, i: (0, 0)
    if final_g is None:
        first = 0
        body = _out_body
        extra_in, extra_specs = (), ()
        out_rows = s
        aliases = {0: 0}
    else:
        first = n_ctx_tiles
        body = _out_final_body
        extra_in, extra_specs = (final_g,), (pl.BlockSpec((1, d), const2),)
        out_rows = s - n_ctx_tiles * tm
        aliases = {}
    row = lambda bb, i: (bb, i + first, 0)
    modrow = lambda bb, i: (jnp.where(i + first < n_ctx_tiles, bsz, bb), 0, 0)
    return pl.pallas_call(
        body,
        out_shape=jax.ShapeDtypeStruct((b, out_rows, d), F32),
        grid=(b, s // tm - first),
        in_specs=[pl.BlockSpec((1, tm, d), row),
                  pl.BlockSpec((1, tm, width), row),
                  pl.BlockSpec(w_bf.shape, const2),
                  pl.BlockSpec((1, 3, d), modrow),
                  *extra_specs],
        out_specs=pl.BlockSpec((1, tm, d), lambda bb, i: (bb, i, 0)),
        input_output_aliases=aliases,
        compiler_params=_cparams(("parallel", "arbitrary")),
        name="out_proj",
    )(xc, a, w_bf, mod, *extra_in)


B_QLORA_PAD = LANES


def _proj_b_body(x_ref, mod_ref, g_ref, w_ref, qn_ref, wuq_ref, kvn_ref, wukv_ref,
                 cq_ref, sq_ref, ck_ref, sk_ref, q_ref, k_ref, v_ref, sg_ref):
    tm = x_ref.shape[1]
    q_lora = qn_ref.shape[1]
    kv_lora = kvn_ref.shape[1]
    h = _norm_mod(x_ref[0], g_ref[...], mod_ref[0]).astype(BF16)
    y = _dot(h, w_ref[...])
    lane = lax.broadcasted_iota(jnp.int32, (tm, LANES), 1)
    even = (((lane - B_NOPE) // (B_ROPE // 4)) % 2 == 0)
    low = lane < B_NOPE

    def rms(u, gain):
        return u * lax.rsqrt(jnp.mean(u * u, axis=-1, keepdims=True) + EPS) * gain

    cq = rms(y[:, :q_lora], qn_ref[...]).astype(BF16)
    ckv = rms(y[:, q_lora:q_lora + kv_lora], kvn_ref[...]).astype(BF16)
    krg = _rope128(y[:, q_lora + kv_lora:q_lora + kv_lora + LANES], ck_ref[...], sk_ref[...],
                   B_ROPE // 4, even)
    qf = _dot(cq, wuq_ref[...])
    kv = _dot(ckv, wukv_ref[...])
    cqt, sqt = cq_ref[...], sq_ref[...]
    zero = jnp.zeros((tm, LANES), F32)
    for hd in range(B_HEADS):
        sl = slice(hd * LANES, (hd + 1) * LANES)
        q_ref[0, :, sl] = _rope128(qf[:, sl], cqt, sqt, B_ROPE // 4, even).astype(BF16)
        grp = kv[:, sl]
        k_ref[0, :, sl] = jnp.where(low, grp, krg).astype(BF16)
        if hd % 2 == 0:
            vpad = jnp.where(low, pltpu.roll(grp, B_NOPE, 1), zero)
        else:
            vpad = jnp.where(low, zero, grp)
        v_ref[0, :, sl] = vpad.astype(BF16)
    sg_ref[0] = _silu(y[:, q_lora + kv_lora + LANES:]).astype(BF16)


def _proj_b(xc, mod, g, w_bf, qn, wuq_bf, kvn, wukv_bf, tabs, n_ctx_tiles, bsz):
    b, s, d = xc.shape
    tm = ROW_TILE
    hw = B_HEADS * LANES
    gw = B_HEADS * B_V
    row = lambda bb, i: (bb, i, 0)
    modrow = lambda bb, i: (jnp.where(i < n_ctx_tiles, bsz, bb), 0, 0)
    const2 = lambda bb, i: (0, 0)
    tab = pl.BlockSpec((tm, LANES), lambda bb, i: (i, 0))
    big = jax.ShapeDtypeStruct((b, s, hw), BF16)
    full = lambda a: pl.BlockSpec(a.shape, const2)
    return pl.pallas_call(
        _proj_b_body,
        out_shape=(big, big, big, jax.ShapeDtypeStruct((b, s, gw), BF16)),
        grid=(b, s // tm),
        in_specs=[pl.BlockSpec((1, tm, d), row),
                  pl.BlockSpec((1, 3, d), modrow),
                  pl.BlockSpec((1, d), const2),
                  full(w_bf), full(qn), full(wuq_bf), full(kvn), full(wukv_bf),
                  tab, tab, tab, tab],
        out_specs=(pl.BlockSpec((1, tm, hw), row), pl.BlockSpec((1, tm, hw), row),
                   pl.BlockSpec((1, tm, hw), row), pl.BlockSpec((1, tm, gw), row)),
        compiler_params=_cparams(("parallel", "arbitrary")),
        name="proj_b",
    )(xc, mod, g, w_bf, qn, wuq_bf, kvn, wukv_bf, *tabs)


def _attn_b_body(q_ref, k_ref, v_ref, sg_ref, o_ref, *, ctx_len, n_ctx_tiles):
    i = pl.program_id(2)

    def attend(nk):
        acc = None
        for hd in range(2):
            sl = slice(hd * LANES, (hd + 1) * LANES)
            s = _dot_nt(q_ref[0, :, sl], k_ref[0, 0:nk, sl])
            m = jnp.max(s, axis=-1, keepdims=True)
            p = jnp.exp(s - m)
            l = jnp.sum(p, axis=-1, keepdims=True)
            pv = _dot(p.astype(BF16), v_ref[0, 0:nk, sl]) * (1.0 / l)
            acc = pv if acc is None else acc + pv
        o_ref[0] = (acc * sg_ref[0].astype(F32)).astype(BF16)

    pl.when(i < n_ctx_tiles)(lambda: attend(ctx_len))
    pl.when(i >= n_ctx_tiles)(lambda: attend(k_ref.shape[1]))


def _attn_b(q, kpad, vpad, sg, ctx_len):
    b, s, hw = q.shape
    tq = ROW_TILE
    pairs = B_HEADS // 2
    body = functools.partial(_attn_b_body, ctx_len=ctx_len, n_ctx_tiles=ctx_len // tq)
    qmap = lambda bb, hp, i: (bb, i, hp)
    kmap = lambda bb, hp, i: (bb, 0, hp)
    return pl.pallas_call(
        body,
        out_shape=jax.ShapeDtypeStruct((b, s, B_HEADS * B_V), BF16),
        grid=(b, pairs, s // tq),
        in_specs=[pl.BlockSpec((1, tq, 2 * LANES), qmap),
                  pl.BlockSpec((1, s, 2 * LANES), kmap),
                  pl.BlockSpec((1, s, 2 * LANES), kmap),
                  pl.BlockSpec((1, tq, LANES), qmap)],
        out_specs=pl.BlockSpec((1, tq, LANES), qmap),
        compiler_params=_cparams(("parallel", "arbitrary", "arbitrary")),
        name="attn_b",
    )(q, kpad, vpad, sg)


C_ROW_TILE = 256


def _proj_c_body(x_ref, modc_ref, modx_ref, g_ref, wdt_ref, dtb_ref, w_ref, cw_ref, cb_ref,
                 z_ref, xbc_ref, dt_ref, ht_ref, *, ctx_len, z_tiles):
    j = pl.program_id(1)
    s = x_ref.shape[1]
    tm = ROW_TILE

    @pl.when(j == 0)
    def _():
        for r in range(s // tm):
            mod = modc_ref[0] if r * tm < ctx_len else modx_ref[0]
            hh = _norm_mod(x_ref[0, r * tm:(r + 1) * tm, :], g_ref[...], mod)
            ht_ref[:, r * tm:(r + 1) * tm] = hh.T.astype(BF16)
        dt = _dot(wdt_ref[...], ht_ref[...]) + dtb_ref[...]
        dt_ref[0] = jnp.maximum(dt, 0.0) + jnp.log1p(jnp.exp(-jnp.abs(dt)))

    @pl.when(j < z_tiles)
    def _():
        z_ref[0] = _dot(w_ref[...], ht_ref[...]).astype(BF16)

    @pl.when(j >= z_tiles)
    def _():
        u = _dot(w_ref[...], ht_ref[...])
        t = lax.broadcasted_iota(jnp.int32, (1, s), 1)
        has_prev = (t != 0) & (t != ctx_len)
        has_next = (t != ctx_len - 1) & (t != s - 1)
        up = jnp.where(has_prev, pltpu.roll(u, 1, 1), 0.0)
        un = jnp.where(has_next, pltpu.roll(u, s - 1, 1), 0.0)
        cw = cw_ref[...]
        v = cw[:, 0:1] * up + cw[:, 1:2] * u + cw[:, 2:3] * un + cb_ref[...]
        xbc_ref[0] = _silu(v).astype(BF16)


def _proj_c(xc, mod, g, wt_bf, wdt_bf, dtb, cwt, cbt, ctx_len, bsz, inner):
    b, s, d = xc.shape
    tc = C_ROW_TILE
    nrows = wt_bf.shape[0]
    z_tiles = inner // tc
    n_tiles = nrows // tc
    conv_dim = nrows - inner
    ndt = wdt_bf.shape[0]
    body = functools.partial(_proj_c_body, ctx_len=ctx_len, z_tiles=z_tiles)
    const2 = lambda bb, j: (0, 0)
    return pl.pallas_call(
        body,
        out_shape=(jax.ShapeDtypeStruct((b, inner, s), BF16),
                   jax.ShapeDtypeStruct((b, conv_dim, s), BF16),
                   jax.ShapeDtypeStruct((b, ndt, s), F32)),
        grid=(b, n_tiles),
        in_specs=[pl.BlockSpec((1, s, d), lambda bb, j: (bb, 0, 0)),
                  pl.BlockSpec((1, 3, d), lambda bb, j: (bsz, 0, 0)),
                  pl.BlockSpec((1, 3, d), lambda bb, j: (bb, 0, 0)),
                  pl.BlockSpec((1, d), const2),
                  pl.BlockSpec((ndt, d), const2),
                  pl.BlockSpec((ndt, 1), const2),
                  pl.BlockSpec((tc, d), lambda bb, j: (j, 0)),
                  pl.BlockSpec((tc, 3), lambda bb, j: (jnp.maximum(j - z_tiles, 0), 0)),
                  pl.BlockSpec((tc, 1), lambda bb, j: (jnp.maximum(j - z_tiles, 0), 0))],
        out_specs=(pl.BlockSpec((1, tc, s), lambda bb, j: (bb, jnp.minimum(j, z_tiles - 1), 0)),
                   pl.BlockSpec((1, tc, s), lambda bb, j: (bb, jnp.maximum(j - z_tiles, 0), 0)),
                   pl.BlockSpec((1, ndt, s), lambda bb, j: (bb, 0, 0))),
        scratch_shapes=[pltpu.VMEM((d, s), BF16)],
        compiler_params=_cparams(("parallel", "arbitrary")),
        name="proj_c",
    )(xc, mod, mod, g, wdt_bf, dtb, wt_bf, cwt, cbt)


def _ssd_chunk(t, n_chunks, ctx_chunks):
    tb = t - n_chunks
    cb = jnp.where(tb < ctx_chunks, ctx_chunks - 1 - tb, n_chunks - 1 - (tb - ctx_chunks))
    return jnp.where(t < n_chunks, t, cb)


def _rep_rows(a, reps):
    r = a.shape[0]
    return jnp.broadcast_to(a[:, None, :], (r, reps, a.shape[1])).reshape(r * reps, a.shape[1])


def _ssd_body(z_ref, xbc_ref, dt_ref, a_ref, dsk_ref, nw_ref, o_ref, yf_ref, st_ref,
              *, n_chunks, ctx_chunks, heads, inner):
    t = pl.program_id(1)
    q = C_CHUNK
    hpg = heads // C_GROUPS
    gw = hpg * C_HEAD_DIM
    bwd = t >= n_chunks
    d = bwd.astype(jnp.int32)
    c = _ssd_chunk(t, n_chunks, ctx_chunks)

    @pl.when((t == 0) | (t == n_chunks))
    def _():
        st_ref[...] = jnp.zeros_like(st_ref)

    dtd = dt_ref[0, pl.ds(pl.multiple_of(d * heads, heads), heads), :]
    a = dtd * a_ref[pl.ds(pl.multiple_of(d * heads, heads), heads), :]
    ii = lax.broadcasted_iota(jnp.int32, (q, q), 0)
    jj = lax.broadcasted_iota(jnp.int32, (q, q), 1)
    tri = (ii <= jj).astype(BF16)
    a_hi, a_lo = _split2(a)
    a_lo2 = (a - a_hi.astype(F32) - a_lo.astype(F32)).astype(BF16)
    cum = _dot(a_hi, tri) + _dot(a_lo, tri) + _dot(a_lo2, tri)
    tot = jnp.broadcast_to(cum[:, q - 1:q], cum.shape)
    u = jnp.where(bwd, tot - cum + a, cum)
    eu = jnp.exp(u)
    dend = jnp.exp(tot - u)
    etot = jnp.exp(tot)
    sgn = 1 - 2 * d
    keep = (jj - ii) * sgn >= 0

    for g in range(C_GROUPS):
        r0 = g * gw
        hs = slice(g * hpg, (g + 1) * hpg)
        b_t = xbc_ref[0, inner + g * C_STATE: inner + (g + 1) * C_STATE, :]
        c_t = xbc_ref[0, inner + (C_GROUPS + g) * C_STATE: inner + (C_GROUPS + g + 1) * C_STATE, :]
        b_g = b_t.astype(F32).T.astype(BF16)
        cb_t = _dot(b_g, c_t)
        xs = xbc_ref[0, r0:r0 + gw, :].astype(F32)
        x_g = xs * _rep_rows(dtd[hs], C_HEAD_DIM)
        st = st_ref[r0:r0 + gw, :]
        y_off = _dot(st.astype(BF16), c_t) * _rep_rows(eu[hs], C_HEAD_DIM)
        st_ref[r0:r0 + gw, :] = (_rep_rows(etot[hs], C_HEAD_DIM) * st
                                 + _dot((x_g * _rep_rows(dend[hs], C_HEAD_DIM)).astype(BF16), b_g))
        x_bf = x_g.astype(BF16)
        ys = []
        for hh in range(hpg):
            hd = g * hpg + hh
            r = jnp.broadcast_to(u[hd:hd + 1, :], (q, q))
            e = jnp.where(keep, r - r.T, NEG)
            m_t = (jnp.exp(e) * cb_t).astype(BF16)
            ys.append(_dot(x_bf[hh * C_HEAD_DIM:(hh + 1) * C_HEAD_DIM], m_t))
        y_g = jnp.concatenate(ys, axis=0) + y_off

        @pl.when(jnp.logical_not(bwd))
        def _():
            yf_ref[c, r0:r0 + gw, :] = y_g

        @pl.when(bwd)
        def _():
            y = yf_ref[c, r0:r0 + gw, :] + y_g + dsk_ref[r0:r0 + gw, :] * xs
            y = y * _silu(z_ref[0, r0:r0 + gw, :].astype(F32))
            ms = jnp.mean(y * y, axis=0, keepdims=True)
            y = y * lax.rsqrt(ms + EPS) * nw_ref[r0:r0 + gw, :]
            o_ref[0, :, r0:r0 + gw] = y.T.astype(BF16)


def _ssd(z_t, xbc_t, dt_t, a_tab, dsk_tab, nw_tab, ctx_len):
    b, inner, s = z_t.shape
    conv_dim = xbc_t.shape[1]
    heads = inner // C_HEAD_DIM
    n_chunks = s // C_CHUNK
    ctx_chunks = ctx_len // C_CHUNK
    body = functools.partial(_ssd_body, n_chunks=n_chunks, ctx_chunks=ctx_chunks, heads=heads,
                             inner=inner)
    cmap = lambda bb, t: (bb, 0, _ssd_chunk(t, n_chunks, ctx_chunks))
    omap = lambda bb, t: (bb, _ssd_chunk(jnp.maximum(t, n_chunks), n_chunks, ctx_chunks), 0)
    const2 = lambda bb, t: (0, 0)
    return pl.pallas_call(
        body,
        out_shape=jax.ShapeDtypeStruct((b, s, inner), BF16),
        grid=(b, 2 * n_chunks),
        in_specs=[pl.BlockSpec((1, inner, C_CHUNK), cmap),
                  pl.BlockSpec((1, conv_dim, C_CHUNK), cmap),
                  pl.BlockSpec((1, 2 * heads, C_CHUNK), cmap),
                  pl.BlockSpec((2 * heads, LANES), const2),
                  pl.BlockSpec((inner, LANES), const2),
                  pl.BlockSpec((inner, LANES), const2)],
        out_specs=pl.BlockSpec((1, C_CHUNK, inner), omap),
        scratch_shapes=[pltpu.VMEM((n_chunks, inner, C_CHUNK), F32),
                        pltpu.VMEM((inner, C_STATE), F32)],
        compiler_params=_cparams(("parallel", "arbitrary")),
        name="ssd_scan",
    )(z_t, xbc_t, dt_t, a_tab, dsk_tab, nw_tab)


def _lanes(v):
    return jnp.broadcast_to(v.astype(F32)[:, None], (v.shape[0], LANES))


def kernel(x, c, ctx, c_ctx, ada_w, ada_b, norm_g, final_g, a_w_in, a_sink, a_w_out, b_w_in, b_q_norm, b_w_uq, b_kv_norm, b_w_ukv, b_w_out, c_w_in, c_conv_w, c_conv_b, c_dt_bias, c_a_log, c_d, c_norm, c_w_out):
    bsz, t_len, d = x.shape
    ctx_len = ctx.shape[1]
    depth = ada_w.shape[0]
    assert ctx_len % ROW_TILE == 0 and t_len % ROW_TILE == 0 and t_len % GRID_W == 0
    n_ctx_tiles = ctx_len // ROW_TILE
    rows = t_len // GRID_W

    xc = jnp.concatenate([ctx, x], axis=1)

    r_pad = -(-(bsz + 1) // 8) * 8
    cond = jnp.concatenate([c, c_ctx[None], jnp.zeros((r_pad - bsz - 1, d), F32)], axis=0)
    mods = _adaln_all(cond, ada_w, ada_b).reshape(depth, r_pad, 3, d)

    lane = jnp.arange(LANES)
    a_scale = A_HEAD_DIM ** -0.5
    a_lane_dim = lane % A_HEAD_DIM
    a_tabs = (_rope_tables(rows, A_HEAD_DIM, ctx_len, a_lane_dim, a_scale)
              + _rope_tables(rows, A_HEAD_DIM, ctx_len, a_lane_dim, 1.0))
    b_scale = (B_NOPE + B_ROPE) ** -0.5
    b_lane_dim = jnp.where((lane >= B_NOPE) & (lane < B_NOPE + B_ROPE), lane - B_NOPE, -1)
    bq_cos, bq_sin = _rope_tables(rows, B_ROPE, ctx_len, b_lane_dim, 1.0)
    b_tabs = (bq_cos * b_scale, bq_sin * b_scale, bq_cos, bq_sin)

    out = None
    for i in range(depth):
        kind = i % N_MIXERS
        j = i // N_MIXERS
        last = i == depth - 1
        mod = mods[i]
        g = norm_g[i][None]
        if kind == 0:
            q, kp, vp, sg = _proj_a(xc, mod, g, a_w_in[j].astype(BF16), a_tabs, n_ctx_tiles, bsz)
            act = _attn_a(q, kp, vp, sg, a_sink[j].astype(F32), ctx_len, not last)
            w_out = a_w_out[j]
        elif kind == 1:
            w = b_w_in[j]
            q_lora = b_q_norm.shape[1]
            kv_lora = b_kv_norm.shape[1]
            k0 = q_lora + kv_lora
            w_pad = jnp.concatenate(
                [w[:, :k0], jnp.zeros((d, B_NOPE), F32), w[:, k0:k0 + B_ROPE],
                 jnp.zeros((d, LANES - B_NOPE - B_ROPE), F32), w[:, k0 + B_ROPE:]], axis=1)
            wuq = b_w_uq[j].reshape(q_lora, B_HEADS, B_NOPE + B_ROPE)
            wuq = jnp.pad(wuq, ((0, 0), (0, 0), (0, LANES - B_NOPE - B_ROPE))).reshape(q_lora, B_HEADS * LANES)
            q, kp, vp, sg = _proj_b(xc, mod, g, w_pad.astype(BF16), b_q_norm[j][None], wuq.astype(BF16),
                                    b_kv_norm[j][None], b_w_ukv[j].astype(BF16), b_tabs,
                                    n_ctx_tiles, bsz)
            act = _attn_b(q, kp, vp, sg, ctx_len)
            w_out = b_w_out[j]
        else:
            inner = c_norm.shape[1]
            heads = inner // C_HEAD_DIM
            conv_dim = c_conv_w.shape[2]
            w = c_w_in[j]
            wt = w[:, :inner + conv_dim].T.astype(BF16)
            wdt = w[:, inner + conv_dim:].T.astype(BF16)
            z_t, xbc_t, dt_t = _proj_c(xc, mod, g, wt, wdt, c_dt_bias[j].reshape(2 * heads, 1),
                                       c_conv_w[j].T, c_conv_b[j][:, None], ctx_len, bsz, inner)
            a_tab = _lanes(-jnp.exp(c_a_log[j].astype(F32)).reshape(2 * heads))
            dsk_tab = _lanes(jnp.repeat(c_d[j], C_HEAD_DIM))
            act = _ssd(z_t, xbc_t, dt_t, a_tab, dsk_tab, _lanes(c_norm[j]), ctx_len)
            w_out = c_w_out[j]
        if last:
            out = _out_proj(xc, act, w_out.astype(BF16), mod, n_ctx_tiles, bsz, final_g=final_g[None])
        else:
            xc = _out_proj(xc, act, w_out.astype(BF16), mod, n_ctx_tiles, bsz)
    return out
```

```python
import functools
import math

import jax
import jax.numpy as jnp
from jax import lax
from jax.experimental import pallas as pl
from jax.experimental.pallas import tpu as pltpu

F32 = jnp.float32
BF16 = jnp.bfloat16

GRID_W = 64
N_MIXERS = 3
ROPE_BASE = 10000.0
EPS = 1e-6
BLOCK = 128
WINDOW = 128

A_HEADS = 16
A_KV_HEADS = 4
A_GROUP = A_HEADS // A_KV_HEADS
A_HEAD_DIM = 64

B_HEADS = 16
B_NOPE = 64
B_ROPE = 32
B_V = 64

C_HEAD_DIM = 64
C_GROUPS = 4
C_STATE = 128
C_CHUNK = 128

LANES = 128
BF16_ROWS = 16
ROW_TILE = 256
VMEM_LIMIT = 56 * 1024 * 1024
NEG = -1e30
LOG2E = math.log2(math.e)


def _cparams(sem):
    return pltpu.CompilerParams(dimension_semantics=sem, vmem_limit_bytes=VMEM_LIMIT)


def _dot(a, b):
    return jnp.dot(a, b, preferred_element_type=F32)


def _dot_nt(a, b):
    return lax.dot_general(a, b, (((1,), (1,)), ((), ())), preferred_element_type=F32)


def _silu(v):
    return v / (1.0 + jnp.exp(-v))


def _split2(a):
    hi = a.astype(BF16)
    lo = (a - hi.astype(F32)).astype(BF16)
    return hi, lo


def _dot_f32(a, b):
    a_hi, a_lo = _split2(a)
    b_hi, b_lo = _split2(b)
    return _dot(a_hi, b_hi) + _dot(a_hi, b_lo) + _dot(a_lo, b_hi)


def _norm_mod(x, g, mod):
    ms = jnp.mean(x * x, axis=-1, keepdims=True)
    y = x * lax.rsqrt(ms + EPS) * g
    return y * (1.0 + mod[1:2]) + mod[0:1]


def _with_ones(v_t):
    return jnp.concatenate([v_t, jnp.ones((BF16_ROWS, v_t.shape[1]), BF16)], axis=0)


def _adaln_body(cond_ref, w_ref, b_ref, o_ref):
    o_ref[0] = _dot_f32(_silu(cond_ref[...]), w_ref[0]) + b_ref[0]


def _adaln_all(cond, ada_w, ada_b):
    depth, d, _ = ada_w.shape
    r = cond.shape[0]
    return pl.pallas_call(
        _adaln_body,
        out_shape=jax.ShapeDtypeStruct((depth, r, 3 * d), F32),
        grid=(depth, 3),
        in_specs=[pl.BlockSpec((r, d), lambda i, j: (0, 0)),
                  pl.BlockSpec((1, d, d), lambda i, j: (i, 0, j)),
                  pl.BlockSpec((1, 1, d), lambda i, j: (i, 0, j))],
        out_specs=pl.BlockSpec((1, r, d), lambda i, j: (i, 0, j)),
        compiler_params=_cparams(("arbitrary", "arbitrary")),
        name="adaln_mod",
    )(cond, ada_w, ada_b.reshape(depth, 1, 3 * d))


def _rope_tables(rows, dim, ctx_len, lane_of_dim, scale):
    row = jnp.repeat(jnp.arange(rows), GRID_W).astype(F32)
    col = (jnp.arange(rows * GRID_W) % GRID_W).astype(F32)
    nf = dim // 4
    inv = ROPE_BASE ** (-jnp.arange(nf, dtype=F32) / nf)
    ar = row[:, None] * inv
    ac = col[:, None] * inv
    ang = jnp.concatenate([ar, ar, ac, ac], axis=-1)
    cos, sin = jnp.cos(ang), jnp.sin(ang)
    sign = jnp.where((jnp.arange(dim) // nf) % 2 == 0, -1.0, 1.0).astype(F32)
    sin = sin * sign
    idx = jnp.asarray(lane_of_dim)
    valid = (idx >= 0)[None, :]
    cos_l = jnp.where(valid, cos[:, jnp.maximum(idx, 0)], 1.0)
    sin_l = jnp.where(valid, sin[:, jnp.maximum(idx, 0)], 0.0)
    cos_l = jnp.concatenate([jnp.ones((ctx_len, LANES), F32), cos_l], axis=0)
    sin_l = jnp.concatenate([jnp.zeros((ctx_len, LANES), F32), sin_l], axis=0)
    return cos_l * scale, sin_l * scale


def _rope128(u, cos, sin, quarter, even):
    fwd = pltpu.roll(u, quarter, 1)
    bwd = pltpu.roll(u, LANES - quarter, 1)
    return u * cos + jnp.where(even, bwd, fwd) * sin


def _proj_a_body(x_ref, mod_ref, g_ref, w_ref, cq_ref, sq_ref, ck_ref, sk_ref,
                 q_ref, k_ref, vt_ref, sg_ref):
    tm = x_ref.shape[1]
    h = _norm_mod(x_ref[0], g_ref[...], mod_ref[0]).astype(BF16)
    y = _dot(h, w_ref[...])
    lane = lax.broadcasted_iota(jnp.int32, (tm, LANES), 1)
    even = ((lane % A_HEAD_DIM) // (A_HEAD_DIM // 4)) % 2 == 0
    low = lane < A_HEAD_DIM
    qw = A_HEADS * A_HEAD_DIM
    kw = A_KV_HEADS * A_HEAD_DIM
    cq, sq, ck, sk = cq_ref[...], sq_ref[...], ck_ref[...], sk_ref[...]
    for j in range(qw // LANES):
        u = y[:, j * LANES:(j + 1) * LANES]
        q_ref[0, :, j * LANES:(j + 1) * LANES] = _rope128(u, cq, sq, A_HEAD_DIM // 4, even).astype(BF16)
    for j in range(kw // LANES):
        kg = _rope128(y[:, qw + j * LANES: qw + (j + 1) * LANES], ck, sk, A_HEAD_DIM // 4, even)
        swp = pltpu.roll(kg, A_HEAD_DIM, 1)
        k_ref[0, :, (2 * j) * LANES:(2 * j + 1) * LANES] = jnp.where(low, kg, swp).astype(BF16)
        k_ref[0, :, (2 * j + 1) * LANES:(2 * j + 2) * LANES] = jnp.where(low, swp, kg).astype(BF16)
    vt_ref[0] = y[:, qw + kw:qw + 2 * kw].T.astype(BF16)
    sg_ref[0] = _silu(y[:, qw + 2 * kw:]).astype(BF16)


def _proj_a(xc, mod, g, w_bf, tabs, n_ctx_tiles, bsz):
    b, s, d = xc.shape
    tm = ROW_TILE
    qw = A_HEADS * A_HEAD_DIM
    kw = A_KV_HEADS * A_HEAD_DIM
    row = lambda bb, i: (bb, i, 0)
    modrow = lambda bb, i: (jnp.where(i < n_ctx_tiles, bsz, bb), 0, 0)
    const2 = lambda bb, i: (0, 0)
    tab = pl.BlockSpec((tm, LANES), lambda bb, i: (i, 0))
    wide = jax.ShapeDtypeStruct((b, s, qw), BF16)
    return pl.pallas_call(
        _proj_a_body,
        out_shape=(wide, jax.ShapeDtypeStruct((b, s, 2 * kw), BF16),
                   jax.ShapeDtypeStruct((b, kw, s), BF16), wide),
        grid=(b, s // tm),
        in_specs=[pl.BlockSpec((1, tm, d), row),
                  pl.BlockSpec((1, 3, d), modrow),
                  pl.BlockSpec((1, d), const2),
                  pl.BlockSpec(w_bf.shape, const2),
                  tab, tab, tab, tab],
        out_specs=(pl.BlockSpec((1, tm, qw), row), pl.BlockSpec((1, tm, 2 * kw), row),
                   pl.BlockSpec((1, kw, tm), lambda bb, i: (bb, 0, i)), pl.BlockSpec((1, tm, qw), row)),
        compiler_params=_cparams(("parallel", "arbitrary")),
        name="proj_a",
    )(xc, mod, g, w_bf, *tabs)


def _attn_a_body(sink_ref, q_ref, kp_ref, ko_ref, kn_ref, kc_ref, vp_ref, vo_ref, vn_ref, vc_ref,
                 sg_ref, o_ref, s_ref, *, first_blk, ctx_blocks, lat_blocks):
    blk = pl.program_id(1) + first_blk
    jl = blk - ctx_blocks
    nq = BLOCK
    cols = 2 * nq
    colq = lax.broadcasted_iota(jnp.int32, (1, cols), 1)
    lane = lax.broadcasted_iota(jnp.int32, (cols, LANES), 1)

    def heads(local):
        if local:
            key = lax.broadcasted_iota(jnp.int32, (3 * nq, cols), 0)
            qi = lax.broadcasted_iota(jnp.int32, (3 * nq, cols), 1) % nq
            kk = key % nq
            piece = key // nq
            far = 4 * nq
            no_prev = jnp.where(jl > 0, 0, far)
            no_next = jnp.where(jl < lat_blocks - 1, 0, far)
            ok = ((piece == 1)
                  | ((piece == 0) & (kk - qi >= no_prev))
                  | ((piece == 2) & (qi - kk >= no_next)))
        nctx = kc_ref.shape[1]
        units = [(h, half) for h in range(A_KV_HEADS) for half in range(2)]

        def sink_row(u):
            h, half = units[u]
            return jnp.where(colq < nq, sink_ref[h * A_GROUP + half],
                             sink_ref[h * A_GROUP + 2 + half]) * LOG2E

        def scores(u, part):
            h, half = units[u]
            c0 = h * A_GROUP * A_HEAD_DIM
            hl = slice(h * LANES, (h + 1) * LANES)
            qg = jnp.concatenate([q_ref[0, :, c0:c0 + LANES], q_ref[0, :, c0 + LANES:c0 + 2 * LANES]], axis=0)
            sel = (lane < A_HEAD_DIM) if half == 0 else (lane >= A_HEAD_DIM)
            qm = jnp.where(sel, qg, jnp.zeros_like(qg))
            if part == 0:
                s_t = _dot_nt(kc_ref[0, :, hl], qm)
                s_ref[u % 2, 0:nctx, :] = s_t
            else:
                k_l = jnp.concatenate([kp_ref[0, :, hl], ko_ref[0, :, hl], kn_ref[0, :, hl]], axis=0)
                s_t = jnp.where(ok, _dot_nt(k_l, qm), NEG)
                s_ref[u % 2, nctx:nctx + 3 * nq, :] = s_t
            return jnp.max(s_t, axis=0, keepdims=True)

        def weighted(u, part, m):
            h, half = units[u]
            hr = slice(h * A_HEAD_DIM, (h + 1) * A_HEAD_DIM)
            if part == 0:
                v1 = _with_ones(vc_ref[0, hr, :])
                p = jnp.exp2(s_ref[u % 2, 0:nctx, :] - m)
            else:
                v1 = _with_ones(jnp.concatenate([vp_ref[0, hr, :], vo_ref[0, hr, :], vn_ref[0, hr, :]], axis=1))
                p = jnp.exp2(s_ref[u % 2, nctx:nctx + 3 * nq, :] - m)
            return _dot(v1, p.astype(BF16))

        parts = (0, 1) if local else (0,)
        outs = []
        mx = [scores(0, part) for part in parts]
        for u in range(len(units)):
            snk = sink_row(u)
            m = functools.reduce(jnp.maximum, mx + [snk])
            mx = []
            ov = None
            for part in parts:
                piece = weighted(u, part, m)
                ov = piece if ov is None else ov + piece
                if u + 1 < len(units):
                    mx.append(scores(u + 1, part))
            den = ov[A_HEAD_DIM:A_HEAD_DIM + 1] + jnp.exp2(snk - m)
            outs.append(ov[:A_HEAD_DIM] * (1.0 / den))
            if u % 2 == 1:
                h = units[u][0]
                c0 = h * A_GROUP * A_HEAD_DIM
                lo, hi = outs[u - 1], outs[u]
                top = jnp.concatenate([lo[:, :nq], hi[:, :nq]], axis=0).T
                bot = jnp.concatenate([lo[:, nq:], hi[:, nq:]], axis=0).T
                o_ref[0, :, c0:c0 + LANES] = (top * sg_ref[0, :, c0:c0 + LANES].astype(F32)).astype(BF16)
                o_ref[0, :, c0 + LANES:c0 + 2 * LANES] = (
                    bot * sg_ref[0, :, c0 + LANES:c0 + 2 * LANES].astype(F32)).astype(BF16)

    if first_blk >= ctx_blocks:
        heads(True)
    else:
        pl.when(jl >= 0)(lambda: heads(True))
        pl.when(jl < 0)(lambda: heads(False))


def _attn_a(q, kdup, v_t, sg, sink, ctx_len, with_ctx_queries):
    b, s, qw = q.shape
    kw2 = kdup.shape[2]
    kw = v_t.shape[1]
    ctx_blocks = ctx_len // BLOCK
    lat_blocks = (s - ctx_len) // BLOCK
    first_blk = 0 if with_ctx_queries else ctx_blocks
    nblk = s // BLOCK - first_blk

    def lat(i):
        return jnp.maximum(i + first_blk - ctx_blocks, 0)

    def prev_blk(i):
        return ctx_blocks + jnp.maximum(lat(i) - 1, 0)

    def next_blk(i):
        return ctx_blocks + jnp.minimum(lat(i) + 1, lat_blocks - 1)

    own = lambda bb, i: (bb, i + first_blk, 0)
    qblk = pl.BlockSpec((1, BLOCK, qw), own)
    kblk = lambda f: pl.BlockSpec((1, BLOCK, kw2), lambda bb, i: (bb, f(i), 0))
    vblk = lambda f: pl.BlockSpec((1, kw, BLOCK), lambda bb, i: (bb, 0, f(i)))
    body = functools.partial(_attn_a_body, first_blk=first_blk, ctx_blocks=ctx_blocks,
                             lat_blocks=lat_blocks)
    return pl.pallas_call(
        body,
        out_shape=jax.ShapeDtypeStruct((b, s, qw), BF16),
        grid=(b, nblk),
        in_specs=[pl.BlockSpec(memory_space=pltpu.SMEM),
                  qblk,
                  kblk(prev_blk), kblk(lambda i: i + first_blk), kblk(next_blk),
                  pl.BlockSpec((1, ctx_len, kw2), lambda bb, i: (bb, 0, 0)),
                  vblk(prev_blk), vblk(lambda i: i + first_blk), vblk(next_blk),
                  pl.BlockSpec((1, kw, ctx_len), lambda bb, i: (bb, 0, 0)),
                  qblk],
        out_specs=qblk,
        scratch_shapes=[pltpu.VMEM((2, ctx_len + 3 * BLOCK, 2 * BLOCK), F32)],
        compiler_params=_cparams(("parallel", "arbitrary")),
        name="attn_a",
    )(sink, q, kdup, kdup, kdup, kdup, v_t, v_t, v_t, v_t, sg)


def _out_body(x_ref, a_ref, w_ref, mod_ref, o_ref):
    y = _dot(a_ref[0], w_ref[...])
    o_ref[0] = x_ref[0] + mod_ref[0][2:3] * y


def _out_final_body(x_ref, a_ref, w_ref, mod_ref, fg_ref, o_ref):
    y = _dot(a_ref[0], w_ref[...])
    xn = x_ref[0] + mod_ref[0][2:3] * y
    ms = jnp.mean(xn * xn, axis=-1, keepdims=True)
    o_ref[0] = xn * lax.rsqrt(ms + EPS) * fg_ref[...]


def _out_proj(xc, a, w_bf, mod, n_ctx_tiles, bsz, final_g=None):
    b, s, d = xc.shape
    tm = ROW_TILE
    width = a.shape[-1]
    const2 = lambda bb, i: (0, 0)
    if final_g is None:
        first = 0
        body = _out_body
        extra_in, extra_specs = (), ()
        out_rows = s
        aliases = {0: 0}
    else:
        first = n_ctx_tiles
        body = _out_final_body
        extra_in, extra_specs = (final_g,), (pl.BlockSpec((1, d), const2),)
        out_rows = s - n_ctx_tiles * tm
        aliases = {}
    row = lambda bb, i: (bb, i + first, 0)
    modrow = lambda bb, i: (jnp.where(i + first < n_ctx_tiles, bsz, bb), 0, 0)
    return pl.pallas_call(
        body,
        out_shape=jax.ShapeDtypeStruct((b, out_rows, d), F32),
        grid=(b, s // tm - first),
        in_specs=[pl.BlockSpec((1, tm, d), row),
                  pl.BlockSpec((1, tm, width), row),
                  pl.BlockSpec(w_bf.shape, const2),
                  pl.BlockSpec((1, 3, d), modrow),
                  *extra_specs],
        out_specs=pl.BlockSpec((1, tm, d), lambda bb, i: (bb, i, 0)),
        input_output_aliases=aliases,
        compiler_params=_cparams(("parallel", "arbitrary")),
        name="out_proj",
    )(xc, a, w_bf, mod, *extra_in)


def _proj_b_body(x_ref, mod_ref, g_ref, w_ref, qn_ref, wuq_ref, kvn_ref, wukv_ref,
                 cq_ref, sq_ref, ck_ref, sk_ref, q_ref, k_ref, vt_ref, sg_ref):
    tm = x_ref.shape[1]
    q_lora = qn_ref.shape[1]
    kv_lora = kvn_ref.shape[1]
    h = _norm_mod(x_ref[0], g_ref[...], mod_ref[0]).astype(BF16)
    y = _dot(h, w_ref[...])
    lane = lax.broadcasted_iota(jnp.int32, (tm, LANES), 1)
    even = (((lane - B_NOPE) // (B_ROPE // 4)) % 2 == 0)
    low = lane < B_NOPE

    def rms(u, gain):
        return u * lax.rsqrt(jnp.mean(u * u, axis=-1, keepdims=True) + EPS) * gain

    cq = rms(y[:, :q_lora], qn_ref[...]).astype(BF16)
    ckv = rms(y[:, q_lora:q_lora + kv_lora], kvn_ref[...]).astype(BF16)
    krg = _rope128(y[:, q_lora + kv_lora:q_lora + kv_lora + LANES], ck_ref[...], sk_ref[...],
                   B_ROPE // 4, even)
    qf = _dot(cq, wuq_ref[...])
    kv = _dot(ckv, wukv_ref[...])
    knw = B_HEADS * B_NOPE
    cqt, sqt = cq_ref[...], sq_ref[...]
    for hd in range(B_HEADS):
        sl = slice(hd * LANES, (hd + 1) * LANES)
        q_ref[0, :, sl] = _rope128(qf[:, sl], cqt, sqt, B_ROPE // 4, even).astype(BF16)
        grp = kv[:, (hd // 2) * LANES:(hd // 2 + 1) * LANES]
        if hd % 2 == 1:
            grp = pltpu.roll(grp, B_NOPE, 1)
        k_ref[0, :, sl] = jnp.where(low, grp, krg).astype(BF16)
    vt_ref[0] = kv[:, knw:].T.astype(BF16)
    sg_ref[0] = _silu(y[:, q_lora + kv_lora + LANES:]).astype(BF16)


def _proj_b(xc, mod, g, w_bf, qn, wuq_bf, kvn, wukv_bf, tabs, n_ctx_tiles, bsz):
    b, s, d = xc.shape
    tm = ROW_TILE
    hw = B_HEADS * LANES
    gw = B_HEADS * B_V
    row = lambda bb, i: (bb, i, 0)
    modrow = lambda bb, i: (jnp.where(i < n_ctx_tiles, bsz, bb), 0, 0)
    const2 = lambda bb, i: (0, 0)
    tab = pl.BlockSpec((tm, LANES), lambda bb, i: (i, 0))
    big = jax.ShapeDtypeStruct((b, s, hw), BF16)
    full = lambda a: pl.BlockSpec(a.shape, const2)
    return pl.pallas_call(
        _proj_b_body,
        out_shape=(big, big, jax.ShapeDtypeStruct((b, gw, s), BF16), jax.ShapeDtypeStruct((b, s, gw), BF16)),
        grid=(b, s // tm),
        in_specs=[pl.BlockSpec((1, tm, d), row),
                  pl.BlockSpec((1, 3, d), modrow),
                  pl.BlockSpec((1, d), const2),
                  full(w_bf), full(qn), full(wuq_bf), full(kvn), full(wukv_bf),
                  tab, tab, tab, tab],
        out_specs=(pl.BlockSpec((1, tm, hw), row), pl.BlockSpec((1, tm, hw), row),
                   pl.BlockSpec((1, gw, tm), lambda bb, i: (bb, 0, i)), pl.BlockSpec((1, tm, gw), row)),
        compiler_params=_cparams(("parallel", "arbitrary")),
        name="proj_b",
    )(xc, mod, g, w_bf, qn, wuq_bf, kvn, wukv_bf, *tabs)


B_HEADS_PER_STEP = 4
B_KEY_CHUNK = 256


def _attn_b_body(q_ref, k_ref, vt_ref, sg_ref, o_ref, s_ref, *, ctx_len, n_ctx_tiles):
    i = pl.program_id(2)
    nh = B_HEADS_PER_STEP
    kc = B_KEY_CHUNK

    def attend(nk):
        nc = nk // kc

        def scores(hd, c):
            sl = slice(hd * LANES, (hd + 1) * LANES)
            s_t = _dot_nt(k_ref[0, c * kc:(c + 1) * kc, sl], q_ref[0, :, sl])
            s_ref[hd % 2, c * kc:(c + 1) * kc, :] = s_t
            return jnp.max(s_t, axis=0, keepdims=True)

        def weighted(hd, c, m):
            p = jnp.exp2(s_ref[hd % 2, c * kc:(c + 1) * kc, :] - m).astype(BF16)
            v1 = _with_ones(vt_ref[0, hd * B_V:(hd + 1) * B_V, c * kc:(c + 1) * kc])
            return _dot(v1, p)

        outs = []
        mx = [scores(0, c) for c in range(nc)]
        for hd in range(nh):
            m = functools.reduce(jnp.maximum, mx)
            mx = []
            ov = None
            for c in range(nc):
                part = weighted(hd, c, m)
                ov = part if ov is None else ov + part
                if hd + 1 < nh:
                    mx.append(scores(hd + 1, c))
            outs.append(ov[:B_V] * (1.0 / ov[B_V:B_V + 1]))
        o_t = jnp.concatenate(outs, axis=0)
        o_ref[0] = (o_t.T * sg_ref[0].astype(F32)).astype(BF16)

    pl.when(i < n_ctx_tiles)(lambda: attend(ctx_len))
    pl.when(i >= n_ctx_tiles)(lambda: attend(k_ref.shape[1]))


def _attn_b(q, kpad, v_t, sg, ctx_len):
    b, s, hw = q.shape
    tq = ROW_TILE
    nh = B_HEADS_PER_STEP
    body = functools.partial(_attn_b_body, ctx_len=ctx_len, n_ctx_tiles=ctx_len // tq)
    qmap = lambda bb, hp, i: (bb, i, hp)
    kmap = lambda bb, hp, i: (bb, 0, hp)
    return pl.pallas_call(
        body,
        out_shape=jax.ShapeDtypeStruct((b, s, B_HEADS * B_V), BF16),
        grid=(b, B_HEADS // nh, s // tq),
        in_specs=[pl.BlockSpec((1, tq, nh * LANES), qmap),
                  pl.BlockSpec((1, s, nh * LANES), kmap),
                  pl.BlockSpec((1, nh * B_V, s), lambda bb, hp, i: (bb, hp, 0)),
                  pl.BlockSpec((1, tq, nh * B_V), qmap)],
        out_specs=pl.BlockSpec((1, tq, nh * B_V), qmap),
        scratch_shapes=[pltpu.VMEM((2, s, tq), F32)],
        compiler_params=_cparams(("parallel", "arbitrary", "arbitrary")),
        name="attn_b",
    )(q, kpad, v_t, sg)


C_ROW_TILE = 256


def _proj_c_body(x_ref, modc_ref, modx_ref, g_ref, wdt_ref, dtb_ref, w_ref, cw_ref, cb_ref,
                 z_ref, xbc_ref, dt_ref, ht_ref, *, ctx_len, z_tiles):
    j = pl.program_id(1)
    s = x_ref.shape[1]
    tm = ROW_TILE

    @pl.when(j == 0)
    def _():
        for r in range(s // tm):
            mod = modc_ref[0] if r * tm < ctx_len else modx_ref[0]
            hh = _norm_mod(x_ref[0, r * tm:(r + 1) * tm, :], g_ref[...], mod)
            ht_ref[:, r * tm:(r + 1) * tm] = hh.T.astype(BF16)
        dt = _dot(wdt_ref[...], ht_ref[...]) + dtb_ref[...]
        dt_ref[0] = jnp.maximum(dt, 0.0) + jnp.log1p(jnp.exp(-jnp.abs(dt)))

    @pl.when(j < z_tiles)
    def _():
        z_ref[0] = _dot(w_ref[...], ht_ref[...]).astype(BF16)

    @pl.when(j >= z_tiles)
    def _():
        u = _dot(w_ref[...], ht_ref[...])
        t = lax.broadcasted_iota(jnp.int32, (1, s), 1)
        has_prev = (t != 0) & (t != ctx_len)
        has_next = (t != ctx_len - 1) & (t != s - 1)
        up = jnp.where(has_prev, pltpu.roll(u, 1, 1), 0.0)
        un = jnp.where(has_next, pltpu.roll(u, s - 1, 1), 0.0)
        cw = cw_ref[...]
        v = cw[:, 0:1] * up + cw[:, 1:2] * u + cw[:, 2:3] * un + cb_ref[...]
        xbc_ref[0] = _silu(v).astype(BF16)


def _proj_c(xc, mod, g, wt_bf, wdt_bf, dtb, cwt, cbt, ctx_len, bsz, inner):
    b, s, d = xc.shape
    tc = C_ROW_TILE
    nrows = wt_bf.shape[0]
    z_tiles = inner // tc
    n_tiles = nrows // tc
    conv_dim = nrows - inner
    ndt = wdt_bf.shape[0]
    body = functools.partial(_proj_c_body, ctx_len=ctx_len, z_tiles=z_tiles)
    const2 = lambda bb, j: (0, 0)
    return pl.pallas_call(
        body,
        out_shape=(jax.ShapeDtypeStruct((b, inner, s), BF16),
                   jax.ShapeDtypeStruct((b, conv_dim, s), BF16),
                   jax.ShapeDtypeStruct((b, ndt, s), F32)),
        grid=(b, n_tiles),
        in_specs=[pl.BlockSpec((1, s, d), lambda bb, j: (bb, 0, 0)),
                  pl.BlockSpec((1, 3, d), lambda bb, j: (bsz, 0, 0)),
                  pl.BlockSpec((1, 3, d), lambda bb, j: (bb, 0, 0)),
                  pl.BlockSpec((1, d), const2),
                  pl.BlockSpec((ndt, d), const2),
                  pl.BlockSpec((ndt, 1), const2),
                  pl.BlockSpec((tc, d), lambda bb, j: (j, 0)),
                  pl.BlockSpec((tc, 3), lambda bb, j: (jnp.maximum(j - z_tiles, 0), 0)),
                  pl.BlockSpec((tc, 1), lambda bb, j: (jnp.maximum(j - z_tiles, 0), 0))],
        out_specs=(pl.BlockSpec((1, tc, s), lambda bb, j: (bb, jnp.minimum(j, z_tiles - 1), 0)),
                   pl.BlockSpec((1, tc, s), lambda bb, j: (bb, jnp.maximum(j - z_tiles, 0), 0)),
                   pl.BlockSpec((1, ndt, s), lambda bb, j: (bb, 0, 0))),
        scratch_shapes=[pltpu.VMEM((d, s), BF16)],
        compiler_params=_cparams(("parallel", "arbitrary")),
        name="proj_c",
    )(xc, mod, mod, g, wdt_bf, dtb, wt_bf, cwt, cbt)


def _ssd_chunk(t, n_chunks, ctx_chunks):
    tb = t - n_chunks
    cb = jnp.where(tb < ctx_chunks, ctx_chunks - 1 - tb, n_chunks - 1 - (tb - ctx_chunks))
    return jnp.where(t < n_chunks, t, cb)


def _rep_rows(a, reps):
    r = a.shape[0]
    return jnp.broadcast_to(a[:, None, :], (r, reps, a.shape[1])).reshape(r * reps, a.shape[1])


def _ssd_body(z_ref, xbc_ref, dt_ref, a_ref, dsk_ref, nw_ref, o_ref, yf_ref, st_ref, yb_ref,
              *, n_chunks, ctx_chunks, heads, inner):
    t = pl.program_id(1)
    q = C_CHUNK
    hpg = heads // C_GROUPS
    gw = hpg * C_HEAD_DIM
    bwd = t >= n_chunks
    d = bwd.astype(jnp.int32)
    c = _ssd_chunk(t, n_chunks, ctx_chunks)

    @pl.when((t == 0) | (t == n_chunks))
    def _():
        st_ref[...] = jnp.zeros_like(st_ref)

    dtd = dt_ref[0, pl.ds(pl.multiple_of(d * heads, heads), heads), :]
    a = dtd * a_ref[pl.ds(pl.multiple_of(d * heads, heads), heads), :]
    ii = lax.broadcasted_iota(jnp.int32, (q, q), 0)
    jj = lax.broadcasted_iota(jnp.int32, (q, q), 1)
    tri = (ii <= jj).astype(BF16)
    a_hi, a_lo = _split2(a)
    a_lo2 = (a - a_hi.astype(F32) - a_lo.astype(F32)).astype(BF16)
    cum = _dot(a_hi, tri) + _dot(a_lo, tri) + _dot(a_lo2, tri)
    tot = jnp.broadcast_to(cum[:, q - 1:q], cum.shape)
    u = jnp.where(bwd, tot - cum + a, cum)
    u2 = u * LOG2E
    eu = jnp.exp(u)
    dtdend = dtd * jnp.exp(tot - u)
    etot = jnp.exp(tot)
    sgn = 1 - 2 * d
    keep = (jj - ii) * sgn >= 0

    for g in range(C_GROUPS):
        r0 = g * gw
        hs = slice(g * hpg, (g + 1) * hpg)
        b_t = xbc_ref[0, inner + g * C_STATE: inner + (g + 1) * C_STATE, :]
        c_t = xbc_ref[0, inner + (C_GROUPS + g) * C_STATE: inner + (C_GROUPS + g + 1) * C_STATE, :]
        b_g = b_t.astype(F32).T.astype(BF16)
        cb_t = _dot(b_g, c_t)
        xs = xbc_ref[0, r0:r0 + gw, :].astype(F32)
        x_bf = (xs * _rep_rows(dtd[hs], C_HEAD_DIM)).astype(BF16)
        st = st_ref[r0:r0 + gw, :]
        y_off = _dot(st.astype(BF16), c_t) * _rep_rows(eu[hs], C_HEAD_DIM)
        st_ref[r0:r0 + gw, :] = (_rep_rows(etot[hs], C_HEAD_DIM) * st
                                 + _dot((xs * _rep_rows(dtdend[hs], C_HEAD_DIM)).astype(BF16), b_g))
        for hh in range(hpg):
            hd = g * hpg + hh
            r = jnp.broadcast_to(u2[hd:hd + 1, :], (q, q))
            e = jnp.where(keep, r - r.T, NEG)
            m_t = (jnp.exp2(e) * cb_t).astype(BF16)
            rr = slice(hh * C_HEAD_DIM, (hh + 1) * C_HEAD_DIM)
            yb_ref[r0 + hh * C_HEAD_DIM:r0 + (hh + 1) * C_HEAD_DIM, :] = _dot(x_bf[rr], m_t) + y_off[rr]

    @pl.when(jnp.logical_not(bwd))
    def _():
        yf_ref[c] = yb_ref[...]

    @pl.when(bwd)
    def _():
        for g in range(C_GROUPS):
            r0 = g * gw
            rs = slice(r0, r0 + gw)
            xs = xbc_ref[0, rs, :].astype(F32)
            y = yf_ref[c, rs, :] + yb_ref[rs, :] + dsk_ref[rs, :] * xs
            y = y * _silu(z_ref[0, rs, :].astype(F32))
            ms = jnp.mean(y * y, axis=0, keepdims=True)
            y = y * lax.rsqrt(ms + EPS) * nw_ref[rs, :]
            o_ref[0, :, rs] = y.T.astype(BF16)


def _ssd(z_t, xbc_t, dt_t, a_tab, dsk_tab, nw_tab, ctx_len):
    b, inner, s = z_t.shape
    conv_dim = xbc_t.shape[1]
    heads = inner // C_HEAD_DIM
    n_chunks = s // C_CHUNK
    ctx_chunks = ctx_len // C_CHUNK
    body = functools.partial(_ssd_body, n_chunks=n_chunks, ctx_chunks=ctx_chunks, heads=heads,
                             inner=inner)
    cmap = lambda bb, t: (bb, 0, _ssd_chunk(t, n_chunks, ctx_chunks))
    omap = lambda bb, t: (bb, _ssd_chunk(jnp.maximum(t, n_chunks), n_chunks, ctx_chunks), 0)
    const2 = lambda bb, t: (0, 0)
    return pl.pallas_call(
        body,
        out_shape=jax.ShapeDtypeStruct((b, s, inner), BF16),
        grid=(b, 2 * n_chunks),
        in_specs=[pl.BlockSpec((1, inner, C_CHUNK), cmap),
                  pl.BlockSpec((1, conv_dim, C_CHUNK), cmap),
                  pl.BlockSpec((1, 2 * heads, C_CHUNK), cmap),
                  pl.BlockSpec((2 * heads, LANES), const2),
                  pl.BlockSpec((inner, LANES), const2),
                  pl.BlockSpec((inner, LANES), const2)],
        out_specs=pl.BlockSpec((1, C_CHUNK, inner), omap),
        scratch_shapes=[pltpu.VMEM((n_chunks, inner, C_CHUNK), F32),
                        pltpu.VMEM((inner, C_STATE), F32),
                        pltpu.VMEM((inner, C_CHUNK), F32)],
        compiler_params=_cparams(("parallel", "arbitrary")),
        name="ssd_scan",
    )(z_t, xbc_t, dt_t, a_tab, dsk_tab, nw_tab)


def _lanes(v):
    return jnp.broadcast_to(v.astype(F32)[:, None], (v.shape[0], LANES))


def kernel(x, c, ctx, c_ctx, ada_w, ada_b, norm_g, final_g, a_w_in, a_sink, a_w_out, b_w_in, b_q_norm, b_w_uq, b_kv_norm, b_w_ukv, b_w_out, c_w_in, c_conv_w, c_conv_b, c_dt_bias, c_a_log, c_d, c_norm, c_w_out):
    bsz, t_len, d = x.shape
    ctx_len = ctx.shape[1]
    depth = ada_w.shape[0]
    assert ctx_len % ROW_TILE == 0 and t_len % ROW_TILE == 0 and t_len % GRID_W == 0
    n_ctx_tiles = ctx_len // ROW_TILE
    rows = t_len // GRID_W

    xc = jnp.concatenate([ctx, x], axis=1)

    r_pad = -(-(bsz + 1) // 8) * 8
    cond = jnp.concatenate([c, c_ctx[None], jnp.zeros((r_pad - bsz - 1, d), F32)], axis=0)
    mods = _adaln_all(cond, ada_w, ada_b).reshape(depth, r_pad, 3, d)

    lane = jnp.arange(LANES)
    a_scale = A_HEAD_DIM ** -0.5 * LOG2E
    a_lane_dim = lane % A_HEAD_DIM
    a_tabs = (_rope_tables(rows, A_HEAD_DIM, ctx_len, a_lane_dim, a_scale)
              + _rope_tables(rows, A_HEAD_DIM, ctx_len, a_lane_dim, 1.0))
    b_scale = (B_NOPE + B_ROPE) ** -0.5 * LOG2E
    b_lane_dim = jnp.where((lane >= B_NOPE) & (lane < B_NOPE + B_ROPE), lane - B_NOPE, -1)
    bq_cos, bq_sin = _rope_tables(rows, B_ROPE, ctx_len, b_lane_dim, 1.0)
    b_tabs = (bq_cos * b_scale, bq_sin * b_scale, bq_cos, bq_sin)

    out = None
    for i in range(depth):
        kind = i % N_MIXERS
        j = i // N_MIXERS
        last = i == depth - 1
        mod = mods[i]
        g = norm_g[i][None]
        if kind == 0:
            q, kd, v_t, sg = _proj_a(xc, mod, g, a_w_in[j].astype(BF16), a_tabs, n_ctx_tiles, bsz)
            act = _attn_a(q, kd, v_t, sg, a_sink[j].astype(F32), ctx_len, not last)
            w_out = a_w_out[j]
        elif kind == 1:
            w = b_w_in[j]
            q_lora = b_q_norm.shape[1]
            kv_lora = b_kv_norm.shape[1]
            k0 = q_lora + kv_lora
            w_pad = jnp.concatenate(
                [w[:, :k0], jnp.zeros((d, B_NOPE), F32), w[:, k0:k0 + B_ROPE],
                 jnp.zeros((d, LANES - B_NOPE - B_ROPE), F32), w[:, k0 + B_ROPE:]], axis=1)
            wuq = b_w_uq[j].reshape(q_lora, B_HEADS, B_NOPE + B_ROPE)
            wuq = jnp.pad(wuq, ((0, 0), (0, 0), (0, LANES - B_NOPE - B_ROPE))).reshape(q_lora, B_HEADS * LANES)
            wukv = b_w_ukv[j].reshape(kv_lora, B_HEADS, B_NOPE + B_V)
            wukv = jnp.concatenate([wukv[:, :, :B_NOPE].reshape(kv_lora, B_HEADS * B_NOPE),
                                    wukv[:, :, B_NOPE:].reshape(kv_lora, B_HEADS * B_V)], axis=1)
            q, kp, v_t, sg = _proj_b(xc, mod, g, w_pad.astype(BF16), b_q_norm[j][None], wuq.astype(BF16),
                                     b_kv_norm[j][None], wukv.astype(BF16), b_tabs, n_ctx_tiles, bsz)
            act = _attn_b(q, kp, v_t, sg, ctx_len)
            w_out = b_w_out[j]
        else:
            inner = c_norm.shape[1]
            heads = inner // C_HEAD_DIM
            conv_dim = c_conv_w.shape[2]
            w = c_w_in[j]
            wt = w[:, :inner + conv_dim].T.astype(BF16)
            wdt = w[:, inner + conv_dim:].T.astype(BF16)
            z_t, xbc_t, dt_t = _proj_c(xc, mod, g, wt, wdt, c_dt_bias[j].reshape(2 * heads, 1),
                                       c_conv_w[j].T, c_conv_b[j][:, None], ctx_len, bsz, inner)
            a_tab = _lanes(-jnp.exp(c_a_log[j].astype(F32)).reshape(2 * heads))
            dsk_tab = _lanes(jnp.repeat(c_d[j], C_HEAD_DIM))
            act = _ssd(z_t, xbc_t, dt_t, a_tab, dsk_tab, _lanes(c_norm[j]), ctx_len)
            w_out = c_w_out[j]
        if last:
            out = _out_proj(xc, act, w_out.astype(BF16), mod, n_ctx_tiles, bsz, final_g=final_g[None])
        else:
            xc = _out_proj(xc, act, w_out.astype(BF16), mod, n_ctx_tiles, bsz)
    return out
```

```python
import functools
import math

import jax
import jax.numpy as jnp
from jax import lax
from jax.experimental import pallas as pl
from jax.experimental.pallas import tpu as pltpu

F32 = jnp.float32
BF16 = jnp.bfloat16

GRID_W = 64
N_MIXERS = 3
ROPE_BASE = 10000.0
EPS = 1e-6
BLOCK = 128
WINDOW = 128

A_HEADS = 16
A_KV_HEADS = 4
A_GROUP = A_HEADS // A_KV_HEADS
A_HEAD_DIM = 64

B_HEADS = 16
B_NOPE = 64
B_ROPE = 32
B_V = 64

C_HEAD_DIM = 64
C_GROUPS = 4
C_STATE = 128
C_CHUNK = 128

LANES = 128
BF16_ROWS = 16
ROW_TILE = 256
VMEM_LIMIT = 56 * 1024 * 1024
NEG = -1e30
LOG2E = math.log2(math.e)


def _cparams(sem):
    return pltpu.CompilerParams(dimension_semantics=sem, vmem_limit_bytes=VMEM_LIMIT)


def _dot(a, b):
    return jnp.dot(a, b, preferred_element_type=F32)


def _dot_nt(a, b):
    return lax.dot_general(a, b, (((1,), (1,)), ((), ())), preferred_element_type=F32)


def _silu(v):
    return v / (1.0 + jnp.exp(-v))


def _split2(a):
    hi = a.astype(BF16)
    lo = (a - hi.astype(F32)).astype(BF16)
    return hi, lo


def _dot_f32(a, b):
    a_hi, a_lo = _split2(a)
    b_hi, b_lo = _split2(b)
    return _dot(a_hi, b_hi) + _dot(a_hi, b_lo) + _dot(a_lo, b_hi)


def _norm_mod(x, g, mod):
    ms = jnp.mean(x * x, axis=-1, keepdims=True)
    y = x * lax.rsqrt(ms + EPS) * g
    return y * (1.0 + mod[1:2]) + mod[0:1]


def _with_ones(v_t):
    return jnp.concatenate([v_t, jnp.ones((BF16_ROWS, v_t.shape[1]), BF16)], axis=0)


def _adaln_body(cond_ref, w_ref, b_ref, o_ref):
    o_ref[0] = _dot_f32(_silu(cond_ref[...]), w_ref[0]) + b_ref[0]


def _adaln_all(cond, ada_w, ada_b):
    depth, d, _ = ada_w.shape
    r = cond.shape[0]
    return pl.pallas_call(
        _adaln_body,
        out_shape=jax.ShapeDtypeStruct((depth, r, 3 * d), F32),
        grid=(depth, 3),
        in_specs=[pl.BlockSpec((r, d), lambda i, j: (0, 0)),
                  pl.BlockSpec((1, d, d), lambda i, j: (i, 0, j)),
                  pl.BlockSpec((1, 1, d), lambda i, j: (i, 0, j))],
        out_specs=pl.BlockSpec((1, r, d), lambda i, j: (i, 0, j)),
        compiler_params=_cparams(("arbitrary", "arbitrary")),
        name="adaln_mod",
    )(cond, ada_w, ada_b.reshape(depth, 1, 3 * d))


def _rope_tables(rows, dim, ctx_len, lane_of_dim, scale):
    row = jnp.repeat(jnp.arange(rows), GRID_W).astype(F32)
    col = (jnp.arange(rows * GRID_W) % GRID_W).astype(F32)
    nf = dim // 4
    inv = ROPE_BASE ** (-jnp.arange(nf, dtype=F32) / nf)
    ar = row[:, None] * inv
    ac = col[:, None] * inv
    ang = jnp.concatenate([ar, ar, ac, ac], axis=-1)
    cos, sin = jnp.cos(ang), jnp.sin(ang)
    sign = jnp.where((jnp.arange(dim) // nf) % 2 == 0, -1.0, 1.0).astype(F32)
    sin = sin * sign
    idx = jnp.asarray(lane_of_dim)
    valid = (idx >= 0)[None, :]
    cos_l = jnp.where(valid, cos[:, jnp.maximum(idx, 0)], 1.0)
    sin_l = jnp.where(valid, sin[:, jnp.maximum(idx, 0)], 0.0)
    cos_l = jnp.concatenate([jnp.ones((ctx_len, LANES), F32), cos_l], axis=0)
    sin_l = jnp.concatenate([jnp.zeros((ctx_len, LANES), F32), sin_l], axis=0)
    return cos_l * scale, sin_l * scale


def _rope128(u, cos, sin, quarter, even):
    fwd = pltpu.roll(u, quarter, 1)
    bwd = pltpu.roll(u, LANES - quarter, 1)
    return u * cos + jnp.where(even, bwd, fwd) * sin


def _rope_tables_t(rows, dim, ctx_len, scale):
    row = jnp.repeat(jnp.arange(rows), GRID_W).astype(F32)
    col = (jnp.arange(rows * GRID_W) % GRID_W).astype(F32)
    nf = dim // 4
    inv = ROPE_BASE ** (-jnp.arange(nf, dtype=F32) / nf)
    ar = inv[:, None] * row[None, :]
    ac = inv[:, None] * col[None, :]
    ang = jnp.concatenate([ar, ar, ac, ac], axis=0)
    cos = jnp.concatenate([jnp.ones((dim, ctx_len), F32), jnp.cos(ang)], axis=1)
    sin = jnp.concatenate([jnp.zeros((dim, ctx_len), F32), jnp.sin(ang)], axis=1)
    return cos * scale, sin * scale


def _rope_rows(u, cos_t, sin_t):
    qd = u.shape[0] // 4
    r1, r2, c1, c2 = (u[i * qd:(i + 1) * qd] for i in range(4))
    rot = jnp.concatenate([-r2, r1, -c2, c1], axis=0)
    return u * cos_t + rot * sin_t


def _proj_a_body(x_ref, mod_ref, g_ref, wqt_ref, wvt_ref, w_ref, cqt_ref, sqt_ref, ck_ref, sk_ref,
                 qt_ref, k_ref, vt_ref, sg_ref):
    tm = x_ref.shape[1]
    h = _norm_mod(x_ref[0], g_ref[...], mod_ref[0]).astype(BF16)
    kw = A_KV_HEADS * A_HEAD_DIM
    qt = _dot_nt(wqt_ref[...], h)
    cqt, sqt = cqt_ref[...], sqt_ref[...]
    for hd in range(A_HEADS):
        rs = slice(hd * A_HEAD_DIM, (hd + 1) * A_HEAD_DIM)
        qt_ref[0, rs, :] = _rope_rows(qt[rs], cqt, sqt).astype(BF16)
    vt_ref[0] = _dot_nt(wvt_ref[...], h).astype(BF16)
    y = _dot(h, w_ref[...])
    lane = lax.broadcasted_iota(jnp.int32, (tm, LANES), 1)
    even = ((lane % A_HEAD_DIM) // (A_HEAD_DIM // 4)) % 2 == 0
    ck, sk = ck_ref[...], sk_ref[...]
    for j in range(kw // LANES):
        kg = _rope128(y[:, j * LANES:(j + 1) * LANES], ck, sk, A_HEAD_DIM // 4, even)
        k_ref[0, :, j * LANES:(j + 1) * LANES] = kg.astype(BF16)
    sg_ref[0] = _silu(y[:, kw:]).astype(BF16)


def _proj_a(xc, mod, g, wqt_bf, wvt_bf, w_bf, tabs, n_ctx_tiles, bsz):
    b, s, d = xc.shape
    tm = ROW_TILE
    qw = A_HEADS * A_HEAD_DIM
    kw = A_KV_HEADS * A_HEAD_DIM
    row = lambda bb, i: (bb, i, 0)
    modrow = lambda bb, i: (jnp.where(i < n_ctx_tiles, bsz, bb), 0, 0)
    const2 = lambda bb, i: (0, 0)
    full = lambda a: pl.BlockSpec(a.shape, const2)
    tab = pl.BlockSpec((tm, LANES), lambda bb, i: (i, 0))
    tab_t = pl.BlockSpec((A_HEAD_DIM, tm), lambda bb, i: (0, i))
    wide = jax.ShapeDtypeStruct((b, s, qw), BF16)
    return pl.pallas_call(
        _proj_a_body,
        out_shape=(jax.ShapeDtypeStruct((b, qw, s), BF16), jax.ShapeDtypeStruct((b, s, kw), BF16),
                   jax.ShapeDtypeStruct((b, kw, s), BF16), wide),
        grid=(b, s // tm),
        in_specs=[pl.BlockSpec((1, tm, d), row),
                  pl.BlockSpec((1, 3, d), modrow),
                  pl.BlockSpec((1, d), const2),
                  full(wqt_bf), full(wvt_bf), full(w_bf),
                  tab_t, tab_t, tab, tab],
        out_specs=(pl.BlockSpec((1, qw, tm), lambda bb, i: (bb, 0, i)), pl.BlockSpec((1, tm, kw), row),
                   pl.BlockSpec((1, kw, tm), lambda bb, i: (bb, 0, i)), pl.BlockSpec((1, tm, qw), row)),
        compiler_params=_cparams(("parallel", "arbitrary")),
        name="proj_a",
    )(xc, mod, g, wqt_bf, wvt_bf, w_bf, *tabs)


def _attn_a_body(sink_ref, qt_ref, kp_ref, ko_ref, kn_ref, kc_ref, vp_ref, vo_ref, vn_ref, vc_ref,
                 sg_ref, o_ref, s_ref, *, first_blk, ctx_blocks, lat_blocks):
    blk = pl.program_id(1) + first_blk
    jl = blk - ctx_blocks
    nq = BLOCK
    cols = A_GROUP * nq
    colq = lax.broadcasted_iota(jnp.int32, (1, cols), 1)

    def heads(local):
        if local:
            key = lax.broadcasted_iota(jnp.int32, (3 * nq, cols), 0)
            qi = lax.broadcasted_iota(jnp.int32, (3 * nq, cols), 1) % nq
            kk = key % nq
            piece = key // nq
            far = 4 * nq
            no_prev = jnp.where(jl > 0, 0, far)
            no_next = jnp.where(jl < lat_blocks - 1, 0, far)
            ok = ((piece == 1)
                  | ((piece == 0) & (kk - qi >= no_prev))
                  | ((piece == 2) & (qi - kk >= no_next)))
        nctx = kc_ref.shape[1]

        def sink_row(h):
            row = jnp.zeros((1, cols), F32)
            for g in range(A_GROUP):
                row = jnp.where(colq // nq == g, sink_ref[h * A_GROUP + g], row)
            return row * LOG2E

        def scores(h, part):
            q4 = jnp.concatenate([qt_ref[0, (h * A_GROUP + g) * A_HEAD_DIM:(h * A_GROUP + g + 1) * A_HEAD_DIM, :]
                                  for g in range(A_GROUP)], axis=1)
            zero = jnp.zeros_like(q4)
            rhs = jnp.concatenate([q4, zero] if h % 2 == 0 else [zero, q4], axis=0)
            hl = slice((h // 2) * LANES, (h // 2 + 1) * LANES)
            if part == 0:
                s_t = _dot(kc_ref[0, :, hl], rhs)
                s_ref[h % 2, 0:nctx, :] = s_t
            else:
                k_l = jnp.concatenate([kp_ref[0, :, hl], ko_ref[0, :, hl], kn_ref[0, :, hl]], axis=0)
                s_t = jnp.where(ok, _dot(k_l, rhs), NEG)
                s_ref[h % 2, nctx:nctx + 3 * nq, :] = s_t
            return jnp.max(s_t, axis=0, keepdims=True)

        def weighted(h, part, m):
            hr = slice(h * A_HEAD_DIM, (h + 1) * A_HEAD_DIM)
            if part == 0:
                v1 = _with_ones(vc_ref[0, hr, :])
                p = jnp.exp2(s_ref[h % 2, 0:nctx, :] - m)
            else:
                v1 = _with_ones(jnp.concatenate([vp_ref[0, hr, :], vo_ref[0, hr, :], vn_ref[0, hr, :]], axis=1))
                p = jnp.exp2(s_ref[h % 2, nctx:nctx + 3 * nq, :] - m)
            return _dot(v1, p.astype(BF16))

        parts = (0, 1) if local else (0,)
        mx = [scores(0, part) for part in parts]
        for h in range(A_KV_HEADS):
            snk = sink_row(h)
            m = functools.reduce(jnp.maximum, mx + [snk])
            mx = []
            ov = None
            for part in parts:
                piece = weighted(h, part, m)
                ov = piece if ov is None else ov + piece
                if h + 1 < A_KV_HEADS:
                    mx.append(scores(h + 1, part))
            den = ov[A_HEAD_DIM:A_HEAD_DIM + 1] + jnp.exp2(snk - m)
            out = ov[:A_HEAD_DIM] * (1.0 / den)
            c0 = h * A_GROUP * A_HEAD_DIM
            for pair in range(A_GROUP // 2):
                pc = slice(c0 + pair * LANES, c0 + (pair + 1) * LANES)
                two = jnp.concatenate([out[:, (2 * pair) * nq:(2 * pair + 1) * nq],
                                       out[:, (2 * pair + 1) * nq:(2 * pair + 2) * nq]], axis=0).T
                o_ref[0, :, pc] = (two * sg_ref[0, :, pc].astype(F32)).astype(BF16)

    if first_blk >= ctx_blocks:
        heads(True)
    else:
        pl.when(jl >= 0)(lambda: heads(True))
        pl.when(jl < 0)(lambda: heads(False))


def _attn_a(q_t, k, v_t, sg, sink, ctx_len, with_ctx_queries):
    b, qw, s = q_t.shape
    kw = v_t.shape[1]
    ctx_blocks = ctx_len // BLOCK
    lat_blocks = (s - ctx_len) // BLOCK
    first_blk = 0 if with_ctx_queries else ctx_blocks
    nblk = s // BLOCK - first_blk

    def lat(i):
        return jnp.maximum(i + first_blk - ctx_blocks, 0)

    def prev_blk(i):
        return ctx_blocks + jnp.maximum(lat(i) - 1, 0)

    def next_blk(i):
        return ctx_blocks + jnp.minimum(lat(i) + 1, lat_blocks - 1)

    own = lambda bb, i: (bb, i + first_blk, 0)
    qblk = pl.BlockSpec((1, BLOCK, qw), own)
    kblk = lambda f: pl.BlockSpec((1, BLOCK, kw), lambda bb, i: (bb, f(i), 0))
    vblk = lambda f: pl.BlockSpec((1, kw, BLOCK), lambda bb, i: (bb, 0, f(i)))
    body = functools.partial(_attn_a_body, first_blk=first_blk, ctx_blocks=ctx_blocks,
                             lat_blocks=lat_blocks)
    return pl.pallas_call(
        body,
        out_shape=jax.ShapeDtypeStruct((b, s, qw), BF16),
        grid=(b, nblk),
        in_specs=[pl.BlockSpec(memory_space=pltpu.SMEM),
                  pl.BlockSpec((1, qw, BLOCK), lambda bb, i: (bb, 0, i + first_blk)),
                  kblk(prev_blk), kblk(lambda i: i + first_blk), kblk(next_blk),
                  pl.BlockSpec((1, ctx_len, kw), lambda bb, i: (bb, 0, 0)),
                  vblk(prev_blk), vblk(lambda i: i + first_blk), vblk(next_blk),
                  pl.BlockSpec((1, kw, ctx_len), lambda bb, i: (bb, 0, 0)),
                  qblk],
        out_specs=qblk,
        scratch_shapes=[pltpu.VMEM((2, ctx_len + 3 * BLOCK, A_GROUP * BLOCK), F32)],
        compiler_params=_cparams(("parallel", "arbitrary")),
        name="attn_a",
    )(sink, q_t, k, k, k, k, v_t, v_t, v_t, v_t, sg)


def _out_body(x_ref, a_ref, w_ref, mod_ref, o_ref):
    y = _dot(a_ref[0], w_ref[...])
    o_ref[0] = x_ref[0] + mod_ref[0][2:3] * y


def _out_final_body(x_ref, a_ref, w_ref, mod_ref, fg_ref, o_ref):
    y = _dot(a_ref[0], w_ref[...])
    xn = x_ref[0] + mod_ref[0][2:3] * y
    ms = jnp.mean(xn * xn, axis=-1, keepdims=True)
    o_ref[0] = xn * lax.rsqrt(ms + EPS) * fg_ref[...]


def _out_proj(xc, a, w_bf, mod, n_ctx_tiles, bsz, final_g=None):
    b, s, d = xc.shape
    tm = ROW_TILE
    width = a.shape[-1]
    const2 = lambda bb, i: (0, 0)
    if final_g is None:
        first = 0
        body = _out_body
        extra_in, extra_specs = (), ()
        out_rows = s
        aliases = {0: 0}
    else:
        first = n_ctx_tiles
        body = _out_final_body
        extra_in, extra_specs = (final_g,), (pl.BlockSpec((1, d), const2),)
        out_rows = s - n_ctx_tiles * tm
        aliases = {}
    row = lambda bb, i: (bb, i + first, 0)
    modrow = lambda bb, i: (jnp.where(i + first < n_ctx_tiles, bsz, bb), 0, 0)
    return pl.pallas_call(
        body,
        out_shape=jax.ShapeDtypeStruct((b, out_rows, d), F32),
        grid=(b, s // tm - first),
        in_specs=[pl.BlockSpec((1, tm, d), row),
                  pl.BlockSpec((1, tm, width), row),
                  pl.BlockSpec(w_bf.shape, const2),
                  pl.BlockSpec((1, 3, d), modrow),
                  *extra_specs],
        out_specs=pl.BlockSpec((1, tm, d), lambda bb, i: (bb, i, 0)),
        input_output_aliases=aliases,
        compiler_params=_cparams(("parallel", "arbitrary")),
        name="out_proj",
    )(xc, a, w_bf, mod, *extra_in)


def _proj_b_body(x_ref, mod_ref, g_ref, w_ref, qn_ref, wuqt_ref, kvn_ref, wkn_ref, wvt_ref,
                 cqt_ref, sqt_ref, ck_ref, sk_ref, qt_ref, k_ref, vt_ref, sg_ref, *, q_scale):
    tm = x_ref.shape[1]
    q_lora = qn_ref.shape[1]
    kv_lora = kvn_ref.shape[1]
    h = _norm_mod(x_ref[0], g_ref[...], mod_ref[0]).astype(BF16)
    y = _dot(h, w_ref[...])
    lane = lax.broadcasted_iota(jnp.int32, (tm, LANES), 1)
    even = (((lane - B_NOPE) // (B_ROPE // 4)) % 2 == 0)
    low = lane < B_NOPE

    def rms(u, gain):
        return u * lax.rsqrt(jnp.mean(u * u, axis=-1, keepdims=True) + EPS) * gain

    cq = rms(y[:, :q_lora], qn_ref[...]).astype(BF16)
    ckv = rms(y[:, q_lora:q_lora + kv_lora], kvn_ref[...]).astype(BF16)
    krg = _rope128(y[:, q_lora + kv_lora:q_lora + kv_lora + LANES], ck_ref[...], sk_ref[...],
                   B_ROPE // 4, even)
    qt = _dot_nt(wuqt_ref[...], cq)
    cqt, sqt = cqt_ref[...], sqt_ref[...]
    for hd in range(B_HEADS):
        r0 = hd * LANES
        qt_ref[0, r0:r0 + B_NOPE, :] = (qt[r0:r0 + B_NOPE] * q_scale).astype(BF16)
        qt_ref[0, r0 + B_NOPE:r0 + B_NOPE + B_ROPE, :] = _rope_rows(
            qt[r0 + B_NOPE:r0 + B_NOPE + B_ROPE], cqt, sqt).astype(BF16)
        qt_ref[0, r0 + B_NOPE + B_ROPE:r0 + LANES, :] = qt[r0 + B_NOPE + B_ROPE:r0 + LANES].astype(BF16)
    vt_ref[0] = _dot_nt(wvt_ref[...], ckv).astype(BF16)
    kn = _dot(ckv, wkn_ref[...])
    for hd in range(B_HEADS):
        grp = kn[:, (hd // 2) * LANES:(hd // 2 + 1) * LANES]
        if hd % 2 == 1:
            grp = pltpu.roll(grp, B_NOPE, 1)
        k_ref[0, :, hd * LANES:(hd + 1) * LANES] = jnp.where(low, grp, krg).astype(BF16)
    sg_ref[0] = _silu(y[:, q_lora + kv_lora + LANES:]).astype(BF16)


def _proj_b(xc, mod, g, w_bf, qn, wuqt_bf, kvn, wkn_bf, wvt_bf, tabs, q_scale, n_ctx_tiles, bsz):
    b, s, d = xc.shape
    tm = ROW_TILE
    hw = B_HEADS * LANES
    gw = B_HEADS * B_V
    row = lambda bb, i: (bb, i, 0)
    modrow = lambda bb, i: (jnp.where(i < n_ctx_tiles, bsz, bb), 0, 0)
    const2 = lambda bb, i: (0, 0)
    tab = pl.BlockSpec((tm, LANES), lambda bb, i: (i, 0))
    tab_t = pl.BlockSpec((B_ROPE, tm), lambda bb, i: (0, i))
    big = jax.ShapeDtypeStruct((b, s, hw), BF16)
    full = lambda a: pl.BlockSpec(a.shape, const2)
    return pl.pallas_call(
        functools.partial(_proj_b_body, q_scale=q_scale),
        out_shape=(jax.ShapeDtypeStruct((b, hw, s), BF16), big, jax.ShapeDtypeStruct((b, gw, s), BF16), jax.ShapeDtypeStruct((b, s, gw), BF16)),
        grid=(b, s // tm),
        in_specs=[pl.BlockSpec((1, tm, d), row),
                  pl.BlockSpec((1, 3, d), modrow),
                  pl.BlockSpec((1, d), const2),
                  full(w_bf), full(qn), full(wuqt_bf), full(kvn), full(wkn_bf), full(wvt_bf),
                  tab_t, tab_t, tab, tab],
        out_specs=(pl.BlockSpec((1, hw, tm), lambda bb, i: (bb, 0, i)), pl.BlockSpec((1, tm, hw), row),
                   pl.BlockSpec((1, gw, tm), lambda bb, i: (bb, 0, i)), pl.BlockSpec((1, tm, gw), row)),
        compiler_params=_cparams(("parallel", "arbitrary")),
        name="proj_b",
    )(xc, mod, g, w_bf, qn, wuqt_bf, kvn, wkn_bf, wvt_bf, *tabs)


B_HEADS_PER_STEP = 8
B_KEY_CHUNK = 256


def _attn_b_body(qt_ref, k_ref, vt_ref, sg_ref, o_ref, s_ref, *, ctx_len, n_ctx_tiles):
    i = pl.program_id(2)
    nh = B_HEADS_PER_STEP
    kc = B_KEY_CHUNK

    def attend(nk):
        nc = nk // kc

        def scores(hd, c):
            sl = slice(hd * LANES, (hd + 1) * LANES)
            s_t = _dot(k_ref[0, c * kc:(c + 1) * kc, sl], qt_ref[0, sl, :])
            s_ref[hd % 2, c * kc:(c + 1) * kc, :] = s_t
            return jnp.max(s_t, axis=0, keepdims=True)

        def weighted(hd, c, m):
            p = jnp.exp2(s_ref[hd % 2, c * kc:(c + 1) * kc, :] - m).astype(BF16)
            v1 = _with_ones(vt_ref[0, hd * B_V:(hd + 1) * B_V, c * kc:(c + 1) * kc])
            return _dot(v1, p)

        outs = []
        mx = [scores(0, c) for c in range(nc)]
        for hd in range(nh):
            m = functools.reduce(jnp.maximum, mx)
            mx = []
            ov = None
            for c in range(nc):
                part = weighted(hd, c, m)
                ov = part if ov is None else ov + part
                if hd + 1 < nh:
                    mx.append(scores(hd + 1, c))
            outs.append(ov[:B_V] * (1.0 / ov[B_V:B_V + 1]))
        o_t = jnp.concatenate(outs, axis=0)
        o_ref[0] = (o_t.T * sg_ref[0].astype(F32)).astype(BF16)

    pl.when(i < n_ctx_tiles)(lambda: attend(ctx_len))
    pl.when(i >= n_ctx_tiles)(lambda: attend(k_ref.shape[1]))


def _attn_b(q_t, kpad, v_t, sg, ctx_len):
    b, s, hw = kpad.shape
    tq = ROW_TILE
    nh = B_HEADS_PER_STEP
    body = functools.partial(_attn_b_body, ctx_len=ctx_len, n_ctx_tiles=ctx_len // tq)
    qmap = lambda bb, hp, i: (bb, i, hp)
    kmap = lambda bb, hp, i: (bb, 0, hp)
    return pl.pallas_call(
        body,
        out_shape=jax.ShapeDtypeStruct((b, s, B_HEADS * B_V), BF16),
        grid=(b, B_HEADS // nh, s // tq),
        in_specs=[pl.BlockSpec((1, nh * LANES, tq), lambda bb, hp, i: (bb, hp, i)),
                  pl.BlockSpec((1, s, nh * LANES), kmap),
                  pl.BlockSpec((1, nh * B_V, s), lambda bb, hp, i: (bb, hp, 0)),
                  pl.BlockSpec((1, tq, nh * B_V), qmap)],
        out_specs=pl.BlockSpec((1, tq, nh * B_V), qmap),
        scratch_shapes=[pltpu.VMEM((2, s, tq), F32)],
        compiler_params=_cparams(("parallel", "arbitrary", "arbitrary")),
        name="attn_b",
    )(q_t, kpad, v_t, sg)


C_ROW_TILE = 256


def _proj_c_body(x_ref, modc_ref, modx_ref, g_ref, wdt_ref, dtb_ref, w_ref, cw_ref, cb_ref,
                 z_ref, xbc_ref, dt_ref, ht_ref, *, ctx_len, z_tiles):
    j = pl.program_id(1)
    s = x_ref.shape[1]
    tm = ROW_TILE

    @pl.when(j == 0)
    def _():
        for r in range(s // tm):
            mod = modc_ref[0] if r * tm < ctx_len else modx_ref[0]
            hh = _norm_mod(x_ref[0, r * tm:(r + 1) * tm, :], g_ref[...], mod)
            ht_ref[:, r * tm:(r + 1) * tm] = hh.T.astype(BF16)
        dt = _dot(wdt_ref[...], ht_ref[...]) + dtb_ref[...]
        dt_ref[0] = jnp.maximum(dt, 0.0) + jnp.log1p(jnp.exp(-jnp.abs(dt)))

    @pl.when(j < z_tiles)
    def _():
        z_ref[0] = _dot(w_ref[...], ht_ref[...]).astype(BF16)

    @pl.when(j >= z_tiles)
    def _():
        u = _dot(w_ref[...], ht_ref[...])
        t = lax.broadcasted_iota(jnp.int32, (1, s), 1)
        has_prev = (t != 0) & (t != ctx_len)
        has_next = (t != ctx_len - 1) & (t != s - 1)
        up = jnp.where(has_prev, pltpu.roll(u, 1, 1), 0.0)
        un = jnp.where(has_next, pltpu.roll(u, s - 1, 1), 0.0)
        cw = cw_ref[...]
        v = cw[:, 0:1] * up + cw[:, 1:2] * u + cw[:, 2:3] * un + cb_ref[...]
        xbc_ref[0] = _silu(v).astype(BF16)


def _proj_c(xc, mod, g, wt_bf, wdt_bf, dtb, cwt, cbt, ctx_len, bsz, inner):
    b, s, d = xc.shape
    tc = C_ROW_TILE
    nrows = wt_bf.shape[0]
    z_tiles = inner // tc
    n_tiles = nrows // tc
    conv_dim = nrows - inner
    ndt = wdt_bf.shape[0]
    body = functools.partial(_proj_c_body, ctx_len=ctx_len, z_tiles=z_tiles)
    const2 = lambda bb, j: (0, 0)
    return pl.pallas_call(
        body,
        out_shape=(jax.ShapeDtypeStruct((b, inner, s), BF16),
                   jax.ShapeDtypeStruct((b, conv_dim, s), BF16),
                   jax.ShapeDtypeStruct((b, ndt, s), F32)),
        grid=(b, n_tiles),
        in_specs=[pl.BlockSpec((1, s, d), lambda bb, j: (bb, 0, 0)),
                  pl.BlockSpec((1, 3, d), lambda bb, j: (bsz, 0, 0)),
                  pl.BlockSpec((1, 3, d), lambda bb, j: (bb, 0, 0)),
                  pl.BlockSpec((1, d), const2),
                  pl.BlockSpec((ndt, d), const2),
                  pl.BlockSpec((ndt, 1), const2),
                  pl.BlockSpec((tc, d), lambda bb, j: (j, 0)),
                  pl.BlockSpec((tc, 3), lambda bb, j: (jnp.maximum(j - z_tiles, 0), 0)),
                  pl.BlockSpec((tc, 1), lambda bb, j: (jnp.maximum(j - z_tiles, 0), 0))],
        out_specs=(pl.BlockSpec((1, tc, s), lambda bb, j: (bb, jnp.minimum(j, z_tiles - 1), 0)),
                   pl.BlockSpec((1, tc, s), lambda bb, j: (bb, jnp.maximum(j - z_tiles, 0), 0)),
                   pl.BlockSpec((1, ndt, s), lambda bb, j: (bb, 0, 0))),
        scratch_shapes=[pltpu.VMEM((d, s), BF16)],
        compiler_params=_cparams(("parallel", "arbitrary")),
        name="proj_c",
    )(xc, mod, mod, g, wdt_bf, dtb, wt_bf, cwt, cbt)


def _ssd_chunk(t, n_chunks, ctx_chunks):
    tb = t - n_chunks
    cb = jnp.where(tb < ctx_chunks, ctx_chunks - 1 - tb, n_chunks - 1 - (tb - ctx_chunks))
    return jnp.where(t < n_chunks, t, cb)


def _rep_rows(a, reps):
    r = a.shape[0]
    return jnp.broadcast_to(a[:, None, :], (r, reps, a.shape[1])).reshape(r * reps, a.shape[1])


def _ssd_body(z_ref, xbc_ref, dt_ref, dtn_ref, a_ref, dsk_ref, nw_ref, o_ref, yf_ref, st_ref, yb_ref,
              pro_ref, *, n_chunks, ctx_chunks, heads, inner):
    t = pl.program_id(1)
    q = C_CHUNK
    hpg = heads // C_GROUPS
    gw = hpg * C_HEAD_DIM
    bwd = t >= n_chunks
    d = bwd.astype(jnp.int32)
    c = _ssd_chunk(t, n_chunks, ctx_chunks)
    ii = lax.broadcasted_iota(jnp.int32, (q, q), 0)
    jj = lax.broadcasted_iota(jnp.int32, (q, q), 1)

    def decay_terms(dt_blk_ref, step):
        rev = step >= n_chunks
        rows = pl.ds(pl.multiple_of(rev.astype(jnp.int32) * heads, heads), heads)
        dtd = dt_blk_ref[0, rows, :]
        a = dtd * a_ref[rows, :]
        tri = (ii <= jj).astype(BF16)
        a_hi, a_lo = _split2(a)
        a_lo2 = (a - a_hi.astype(F32) - a_lo.astype(F32)).astype(BF16)
        cum = _dot(a_hi, tri) + _dot(a_lo, tri) + _dot(a_lo2, tri)
        tot = jnp.broadcast_to(cum[:, q - 1:q], cum.shape)
        u = jnp.where(rev, tot - cum + a, cum)
        pro_ref[0] = dtd
        pro_ref[1] = u * LOG2E
        pro_ref[2] = jnp.exp(u)
        pro_ref[3] = dtd * jnp.exp(tot - u)
        pro_ref[4] = jnp.exp(tot)

    @pl.when(t == 0)
    def _():
        decay_terms(dt_ref, t)

    @pl.when((t == 0) | (t == n_chunks))
    def _():
        st_ref[...] = jnp.zeros_like(st_ref)

    dtd, u2, eu, dtdend, etot = (pro_ref[i] for i in range(5))
    decay_terms(dtn_ref, t + 1)
    sgn = 1 - 2 * d
    keep = (jj - ii) * sgn >= 0

    for g in range(C_GROUPS):
        r0 = g * gw
        hs = slice(g * hpg, (g + 1) * hpg)
        b_t = xbc_ref[0, inner + g * C_STATE: inner + (g + 1) * C_STATE, :]
        c_t = xbc_ref[0, inner + (C_GROUPS + g) * C_STATE: inner + (C_GROUPS + g + 1) * C_STATE, :]
        b_g = b_t.astype(F32).T.astype(BF16)
        cb_t = _dot(b_g, c_t)
        xs = xbc_ref[0, r0:r0 + gw, :].astype(F32)
        x_bf = (xs * _rep_rows(dtd[hs], C_HEAD_DIM)).astype(BF16)
        st = st_ref[r0:r0 + gw, :]
        y_off = _dot(st.astype(BF16), c_t) * _rep_rows(eu[hs], C_HEAD_DIM)
        st_ref[r0:r0 + gw, :] = (_rep_rows(etot[hs], C_HEAD_DIM) * st
                                 + _dot((xs * _rep_rows(dtdend[hs], C_HEAD_DIM)).astype(BF16), b_g))
        for hh in range(hpg):
            hd = g * hpg + hh
            r = jnp.broadcast_to(u2[hd:hd + 1, :], (q, q))
            e = jnp.where(keep, r - r.T, NEG)
            m_t = (jnp.exp2(e) * cb_t).astype(BF16)
            rr = slice(hh * C_HEAD_DIM, (hh + 1) * C_HEAD_DIM)
            yb_ref[r0 + hh * C_HEAD_DIM:r0 + (hh + 1) * C_HEAD_DIM, :] = _dot(x_bf[rr], m_t) + y_off[rr]

    @pl.when(jnp.logical_not(bwd))
    def _():
        yf_ref[c] = yb_ref[...] + dsk_ref[...] * xbc_ref[0, 0:inner, :].astype(F32)

    @pl.when(bwd)
    def _():
        for g in range(C_GROUPS):
            r0 = g * gw
            rs = slice(r0, r0 + gw)
            y = yf_ref[c, rs, :] + yb_ref[rs, :]
            y = y * _silu(z_ref[0, rs, :].astype(F32))
            ms = jnp.mean(y * y, axis=0, keepdims=True)
            y = y * lax.rsqrt(ms + EPS) * nw_ref[rs, :]
            o_ref[0, :, rs] = y.T.astype(BF16)


def _ssd(z_t, xbc_t, dt_t, a_tab, dsk_tab, nw_tab, ctx_len):
    b, inner, s = z_t.shape
    conv_dim = xbc_t.shape[1]
    heads = inner // C_HEAD_DIM
    n_chunks = s // C_CHUNK
    ctx_chunks = ctx_len // C_CHUNK
    body = functools.partial(_ssd_body, n_chunks=n_chunks, ctx_chunks=ctx_chunks, heads=heads,
                             inner=inner)
    cmap = lambda bb, t: (bb, 0, _ssd_chunk(t, n_chunks, ctx_chunks))
    nmap = lambda bb, t: (bb, 0, _ssd_chunk(jnp.minimum(t + 1, 2 * n_chunks - 1), n_chunks, ctx_chunks))
    omap = lambda bb, t: (bb, _ssd_chunk(jnp.maximum(t, n_chunks), n_chunks, ctx_chunks), 0)
    const2 = lambda bb, t: (0, 0)
    return pl.pallas_call(
        body,
        out_shape=jax.ShapeDtypeStruct((b, s, inner), BF16),
        grid=(b, 2 * n_chunks),
        in_specs=[pl.BlockSpec((1, inner, C_CHUNK), cmap),
                  pl.BlockSpec((1, conv_dim, C_CHUNK), cmap),
                  pl.BlockSpec((1, 2 * heads, C_CHUNK), cmap),
                  pl.BlockSpec((1, 2 * heads, C_CHUNK), nmap),
                  pl.BlockSpec((2 * heads, LANES), const2),
                  pl.BlockSpec((inner, LANES), const2),
                  pl.BlockSpec((inner, LANES), const2)],
        out_specs=pl.BlockSpec((1, C_CHUNK, inner), omap),
        scratch_shapes=[pltpu.VMEM((n_chunks, inner, C_CHUNK), F32),
                        pltpu.VMEM((inner, C_STATE), F32),
                        pltpu.VMEM((inner, C_CHUNK), F32),
                        pltpu.VMEM((5, heads, C_CHUNK), F32)],
        compiler_params=_cparams(("parallel", "arbitrary")),
        name="ssd_scan",
    )(z_t, xbc_t, dt_t, dt_t, a_tab, dsk_tab, nw_tab)


def _lanes(v):
    return jnp.broadcast_to(v.astype(F32)[:, None], (v.shape[0], LANES))


def kernel(x, c, ctx, c_ctx, ada_w, ada_b, norm_g, final_g, a_w_in, a_sink, a_w_out, b_w_in, b_q_norm, b_w_uq, b_kv_norm, b_w_ukv, b_w_out, c_w_in, c_conv_w, c_conv_b, c_dt_bias, c_a_log, c_d, c_norm, c_w_out):
    bsz, t_len, d = x.shape
    ctx_len = ctx.shape[1]
    depth = ada_w.shape[0]
    assert ctx_len % ROW_TILE == 0 and t_len % ROW_TILE == 0 and t_len % GRID_W == 0
    n_ctx_tiles = ctx_len // ROW_TILE
    rows = t_len // GRID_W

    xc = jnp.concatenate([ctx, x], axis=1)

    r_pad = -(-(bsz + 1) // 8) * 8
    cond = jnp.concatenate([c, c_ctx[None], jnp.zeros((r_pad - bsz - 1, d), F32)], axis=0)
    mods = _adaln_all(cond, ada_w, ada_b).reshape(depth, r_pad, 3, d)

    lane = jnp.arange(LANES)
    a_scale = A_HEAD_DIM ** -0.5 * LOG2E
    a_lane_dim = lane % A_HEAD_DIM
    a_tabs = (_rope_tables_t(rows, A_HEAD_DIM, ctx_len, a_scale)
              + _rope_tables(rows, A_HEAD_DIM, ctx_len, a_lane_dim, 1.0))
    b_scale = (B_NOPE + B_ROPE) ** -0.5 * LOG2E
    b_lane_dim = jnp.where((lane >= B_NOPE) & (lane < B_NOPE + B_ROPE), lane - B_NOPE, -1)
    b_tabs = (_rope_tables_t(rows, B_ROPE, ctx_len, b_scale)
              + _rope_tables(rows, B_ROPE, ctx_len, b_lane_dim, 1.0))

    out = None
    for i in range(depth):
        kind = i % N_MIXERS
        j = i // N_MIXERS
        last = i == depth - 1
        mod = mods[i]
        g = norm_g[i][None]
        if kind == 0:
            w = a_w_in[j]
            qw, kw = A_HEADS * A_HEAD_DIM, A_KV_HEADS * A_HEAD_DIM
            w_tok = jnp.concatenate([w[:, qw:qw + kw], w[:, qw + 2 * kw:]], axis=1)
            q_t, k, v_t, sg = _proj_a(xc, mod, g, w[:, :qw].T.astype(BF16),
                                      w[:, qw + kw:qw + 2 * kw].T.astype(BF16), w_tok.astype(BF16),
                                      a_tabs, n_ctx_tiles, bsz)
            act = _attn_a(q_t, k, v_t, sg, a_sink[j].astype(F32), ctx_len, not last)
            w_out = a_w_out[j]
        elif kind == 1:
            w = b_w_in[j]
            q_lora = b_q_norm.shape[1]
            kv_lora = b_kv_norm.shape[1]
            k0 = q_lora + kv_lora
            w_pad = jnp.concatenate(
                [w[:, :k0], jnp.zeros((d, B_NOPE), F32), w[:, k0:k0 + B_ROPE],
                 jnp.zeros((d, LANES - B_NOPE - B_ROPE), F32), w[:, k0 + B_ROPE:]], axis=1)
            wuq = b_w_uq[j].reshape(q_lora, B_HEADS, B_NOPE + B_ROPE)
            wuq = jnp.pad(wuq, ((0, 0), (0, 0), (0, LANES - B_NOPE - B_ROPE))).reshape(q_lora, B_HEADS * LANES)
            wukv = b_w_ukv[j].reshape(kv_lora, B_HEADS, B_NOPE + B_V)
            wkn = wukv[:, :, :B_NOPE].reshape(kv_lora, B_HEADS * B_NOPE)
            wv = wukv[:, :, B_NOPE:].reshape(kv_lora, B_HEADS * B_V)
            q_t, kp, v_t, sg = _proj_b(xc, mod, g, w_pad.astype(BF16), b_q_norm[j][None],
                                       wuq.T.astype(BF16), b_kv_norm[j][None], wkn.astype(BF16),
                                       wv.T.astype(BF16), b_tabs, b_scale, n_ctx_tiles, bsz)
            act = _attn_b(q_t, kp, v_t, sg, ctx_len)
            w_out = b_w_out[j]
        else:
            inner = c_norm.shape[1]
            heads = inner // C_HEAD_DIM
            conv_dim = c_conv_w.shape[2]
            w = c_w_in[j]
            wt = w[:, :inner + conv_dim].T.astype(BF16)
            wdt = w[:, inner + conv_dim:].T.astype(BF16)
            z_t, xbc_t, dt_t = _proj_c(xc, mod, g, wt, wdt, c_dt_bias[j].reshape(2 * heads, 1),
                                       c_conv_w[j].T, c_conv_b[j][:, None], ctx_len, bsz, inner)
            a_tab = _lanes(-jnp.exp(c_a_log[j].astype(F32)).reshape(2 * heads))
            dsk_tab = _lanes(jnp.repeat(c_d[j], C_HEAD_DIM))
            act = _ssd(z_t, xbc_t, dt_t, a_tab, dsk_tab, _lanes(c_norm[j]), ctx_len)
            w_out = c_w_out[j]
        if last:
            out = _out_proj(xc, act, w_out.astype(BF16), mod, n_ctx_tiles, bsz, final_g=final_g[None])
        else:
            xc = _out_proj(xc, act, w_out.astype(BF16), mod, n_ctx_tiles, bsz)
    return out
```

```python
import functools
import math

import jax
import jax.numpy as jnp
from jax import lax
from jax.experimental import pallas as pl
from jax.experimental.pallas import tpu as pltpu

F32 = jnp.float32
BF16 = jnp.bfloat16

GRID_W = 64
N_MIXERS = 3
ROPE_BASE = 10000.0
EPS = 1e-6
BLOCK = 128
WINDOW = 128

A_HEADS = 16
A_KV_HEADS = 4
A_GROUP = A_HEADS // A_KV_HEADS
A_HEAD_DIM = 64

B_HEADS = 16
B_NOPE = 64
B_ROPE = 32
B_V = 64

C_HEAD_DIM = 64
C_GROUPS = 4
C_STATE = 128
C_CHUNK = 128

LANES = 128
BF16_ROWS = 16
ROW_TILE = 256
VMEM_LIMIT = 56 * 1024 * 1024
NEG = -1e30
LOG2E = math.log2(math.e)


def _cparams(sem):
    return pltpu.CompilerParams(dimension_semantics=sem, vmem_limit_bytes=VMEM_LIMIT)


def _dot(a, b):
    return jnp.dot(a, b, preferred_element_type=F32)


def _dot_nt(a, b):
    return lax.dot_general(a, b, (((1,), (1,)), ((), ())), preferred_element_type=F32)


def _silu(v):
    return v / (1.0 + jnp.exp(-v))


def _split2(a):
    hi = a.astype(BF16)
    lo = (a - hi.astype(F32)).astype(BF16)
    return hi, lo


def _dot_f32(a, b):
    a_hi, a_lo = _split2(a)
    b_hi, b_lo = _split2(b)
    return _dot(a_hi, b_hi) + _dot(a_hi, b_lo) + _dot(a_lo, b_hi)


def _norm_mod(x, g, mod):
    ms = jnp.mean(x * x, axis=-1, keepdims=True)
    y = x * lax.rsqrt(ms + EPS) * g
    return y * (1.0 + mod[1:2]) + mod[0:1]


def _with_ones(v_t):
    return jnp.concatenate([v_t, jnp.ones((BF16_ROWS, v_t.shape[1]), BF16)], axis=0)


def _adaln_body(cond_ref, w_ref, b_ref, o_ref):
    o_ref[0] = _dot_f32(_silu(cond_ref[...]), w_ref[0]) + b_ref[0]


def _adaln_all(cond, ada_w, ada_b):
    depth, d, _ = ada_w.shape
    r = cond.shape[0]
    return pl.pallas_call(
        _adaln_body,
        out_shape=jax.ShapeDtypeStruct((depth, r, 3 * d), F32),
        grid=(depth, 3),
        in_specs=[pl.BlockSpec((r, d), lambda i, j: (0, 0)),
                  pl.BlockSpec((1, d, d), lambda i, j: (i, 0, j)),
                  pl.BlockSpec((1, 1, d), lambda i, j: (i, 0, j))],
        out_specs=pl.BlockSpec((1, r, d), lambda i, j: (i, 0, j)),
        compiler_params=_cparams(("arbitrary", "arbitrary")),
        name="adaln_mod",
    )(cond, ada_w, ada_b.reshape(depth, 1, 3 * d))


def _rope_tables(rows, dim, ctx_len, lane_of_dim, scale):
    row = jnp.repeat(jnp.arange(rows), GRID_W).astype(F32)
    col = (jnp.arange(rows * GRID_W) % GRID_W).astype(F32)
    nf = dim // 4
    inv = ROPE_BASE ** (-jnp.arange(nf, dtype=F32) / nf)
    ar = row[:, None] * inv
    ac = col[:, None] * inv
    ang = jnp.concatenate([ar, ar, ac, ac], axis=-1)
    cos, sin = jnp.cos(ang), jnp.sin(ang)
    sign = jnp.where((jnp.arange(dim) // nf) % 2 == 0, -1.0, 1.0).astype(F32)
    sin = sin * sign
    idx = jnp.asarray(lane_of_dim)
    valid = (idx >= 0)[None, :]
    cos_l = jnp.where(valid, cos[:, jnp.maximum(idx, 0)], 1.0)
    sin_l = jnp.where(valid, sin[:, jnp.maximum(idx, 0)], 0.0)
    cos_l = jnp.concatenate([jnp.ones((ctx_len, LANES), F32), cos_l], axis=0)
    sin_l = jnp.concatenate([jnp.zeros((ctx_len, LANES), F32), sin_l], axis=0)
    return cos_l * scale, sin_l * scale


def _rope128(u, cos, sin, quarter, even):
    fwd = pltpu.roll(u, quarter, 1)
    bwd = pltpu.roll(u, LANES - quarter, 1)
    return u * cos + jnp.where(even, bwd, fwd) * sin


def _rope_tables_t(rows, dim, ctx_len, scale):
    row = jnp.repeat(jnp.arange(rows), GRID_W).astype(F32)
    col = (jnp.arange(rows * GRID_W) % GRID_W).astype(F32)
    nf = dim // 4
    inv = ROPE_BASE ** (-jnp.arange(nf, dtype=F32) / nf)
    ar = inv[:, None] * row[None, :]
    ac = inv[:, None] * col[None, :]
    ang = jnp.concatenate([ar, ar, ac, ac], axis=0)
    cos = jnp.concatenate([jnp.ones((dim, ctx_len), F32), jnp.cos(ang)], axis=1)
    sin = jnp.concatenate([jnp.zeros((dim, ctx_len), F32), jnp.sin(ang)], axis=1)
    return cos * scale, sin * scale


def _rope_rows(u, cos_t, sin_t):
    qd = u.shape[0] // 4
    r1, r2, c1, c2 = (u[i * qd:(i + 1) * qd] for i in range(4))
    rot = jnp.concatenate([-r2, r1, -c2, c1], axis=0)
    return u * cos_t + rot * sin_t


def _proj_a_body(x_ref, mod_ref, g_ref, wqt_ref, wvt_ref, w_ref, cqt_ref, sqt_ref, ck_ref, sk_ref,
                 qt_ref, k_ref, vt_ref, sg_ref):
    tm = x_ref.shape[1]
    h = _norm_mod(x_ref[0], g_ref[...], mod_ref[0]).astype(BF16)
    kw = A_KV_HEADS * A_HEAD_DIM
    qt = _dot_nt(wqt_ref[...], h)
    cqt, sqt = cqt_ref[...], sqt_ref[...]
    for hd in range(A_HEADS):
        rs = slice(hd * A_HEAD_DIM, (hd + 1) * A_HEAD_DIM)
        qt_ref[0, rs, :] = _rope_rows(qt[rs], cqt, sqt).astype(BF16)
    vt_ref[0] = _dot_nt(wvt_ref[...], h).astype(BF16)
    y = _dot(h, w_ref[...])
    lane = lax.broadcasted_iota(jnp.int32, (tm, LANES), 1)
    even = ((lane % A_HEAD_DIM) // (A_HEAD_DIM // 4)) % 2 == 0
    ck, sk = ck_ref[...], sk_ref[...]
    for j in range(kw // LANES):
        kg = _rope128(y[:, j * LANES:(j + 1) * LANES], ck, sk, A_HEAD_DIM // 4, even)
        k_ref[0, :, j * LANES:(j + 1) * LANES] = kg.astype(BF16)
    sg_ref[0] = _silu(y[:, kw:]).astype(BF16)


def _proj_a(xc, mod, g, wqt_bf, wvt_bf, w_bf, tabs, n_ctx_tiles, bsz):
    b, s, d = xc.shape
    tm = ROW_TILE
    qw = A_HEADS * A_HEAD_DIM
    kw = A_KV_HEADS * A_HEAD_DIM
    row = lambda bb, i: (bb, i, 0)
    modrow = lambda bb, i: (jnp.where(i < n_ctx_tiles, bsz, bb), 0, 0)
    const2 = lambda bb, i: (0, 0)
    full = lambda a: pl.BlockSpec(a.shape, const2)
    tab = pl.BlockSpec((tm, LANES), lambda bb, i: (i, 0))
    tab_t = pl.BlockSpec((A_HEAD_DIM, tm), lambda bb, i: (0, i))
    wide = jax.ShapeDtypeStruct((b, s, qw), BF16)
    return pl.pallas_call(
        _proj_a_body,
        out_shape=(jax.ShapeDtypeStruct((b, qw, s), BF16), jax.ShapeDtypeStruct((b, s, kw), BF16),
                   jax.ShapeDtypeStruct((b, kw, s), BF16), wide),
        grid=(b, s // tm),
        in_specs=[pl.BlockSpec((1, tm, d), row),
                  pl.BlockSpec((1, 3, d), modrow),
                  pl.BlockSpec((1, d), const2),
                  full(wqt_bf), full(wvt_bf), full(w_bf),
                  tab_t, tab_t, tab, tab],
        out_specs=(pl.BlockSpec((1, qw, tm), lambda bb, i: (bb, 0, i)), pl.BlockSpec((1, tm, kw), row),
                   pl.BlockSpec((1, kw, tm), lambda bb, i: (bb, 0, i)), pl.BlockSpec((1, tm, qw), row)),
        compiler_params=_cparams(("parallel", "arbitrary")),
        name="proj_a",
    )(xc, mod, g, wqt_bf, wvt_bf, w_bf, *tabs)


def _attn_a_body(sink_ref, qt_ref, kp_ref, ko_ref, kn_ref, kc_ref, vp_ref, vo_ref, vn_ref, vc_ref,
                 sg_ref, x_ref, wo_ref, mod_ref, *rest, first_blk, ctx_blocks, lat_blocks, final):
    if final:
        fg_ref, o_ref, s_ref = rest
    else:
        o_ref, s_ref = rest
    blk = pl.program_id(1) + first_blk
    jl = blk - ctx_blocks
    nq = BLOCK
    cols = A_GROUP * nq
    colq = lax.broadcasted_iota(jnp.int32, (1, cols), 1)

    def heads(local):
        if local:
            key = lax.broadcasted_iota(jnp.int32, (3 * nq, cols), 0)
            qi = lax.broadcasted_iota(jnp.int32, (3 * nq, cols), 1) % nq
            kk = key % nq
            piece = key // nq
            far = 4 * nq
            no_prev = jnp.where(jl > 0, 0, far)
            no_next = jnp.where(jl < lat_blocks - 1, 0, far)
            ok = ((piece == 1)
                  | ((piece == 0) & (kk - qi >= no_prev))
                  | ((piece == 2) & (qi - kk >= no_next)))
        nctx = kc_ref.shape[1]

        def sink_row(h):
            row = jnp.zeros((1, cols), F32)
            for g in range(A_GROUP):
                row = jnp.where(colq // nq == g, sink_ref[h * A_GROUP + g], row)
            return row * LOG2E

        def scores(h, part):
            q4 = jnp.concatenate([qt_ref[0, (h * A_GROUP + g) * A_HEAD_DIM:(h * A_GROUP + g + 1) * A_HEAD_DIM, :]
                                  for g in range(A_GROUP)], axis=1)
            zero = jnp.zeros_like(q4)
            rhs = jnp.concatenate([q4, zero] if h % 2 == 0 else [zero, q4], axis=0)
            hl = slice((h // 2) * LANES, (h // 2 + 1) * LANES)
            if part == 0:
                s_t = _dot(kc_ref[0, :, hl], rhs)
                s_ref[h % 2, 0:nctx, :] = s_t
            else:
                k_l = jnp.concatenate([kp_ref[0, :, hl], ko_ref[0, :, hl], kn_ref[0, :, hl]], axis=0)
                s_t = jnp.where(ok, _dot(k_l, rhs), NEG)
                s_ref[h % 2, nctx:nctx + 3 * nq, :] = s_t
            return jnp.max(s_t, axis=0, keepdims=True)

        def weighted(h, part, m):
            hr = slice(h * A_HEAD_DIM, (h + 1) * A_HEAD_DIM)
            if part == 0:
                v1 = _with_ones(vc_ref[0, hr, :])
                p = jnp.exp2(s_ref[h % 2, 0:nctx, :] - m)
            else:
                v1 = _with_ones(jnp.concatenate([vp_ref[0, hr, :], vo_ref[0, hr, :], vn_ref[0, hr, :]], axis=1))
                p = jnp.exp2(s_ref[h % 2, nctx:nctx + 3 * nq, :] - m)
            return _dot(v1, p.astype(BF16))

        parts = (0, 1) if local else (0,)
        acts = []
        mx = [scores(0, part) for part in parts]
        for h in range(A_KV_HEADS):
            snk = sink_row(h)
            m = functools.reduce(jnp.maximum, mx + [snk])
            mx = []
            ov = None
            for part in parts:
                piece = weighted(h, part, m)
                ov = piece if ov is None else ov + piece
                if h + 1 < A_KV_HEADS:
                    mx.append(scores(h + 1, part))
            den = ov[A_HEAD_DIM:A_HEAD_DIM + 1] + jnp.exp2(snk - m)
            out = ov[:A_HEAD_DIM] * (1.0 / den)
            c0 = h * A_GROUP * A_HEAD_DIM
            for pair in range(A_GROUP // 2):
                pc = slice(c0 + pair * LANES, c0 + (pair + 1) * LANES)
                two = jnp.concatenate([out[:, (2 * pair) * nq:(2 * pair + 1) * nq],
                                       out[:, (2 * pair + 1) * nq:(2 * pair + 2) * nq]], axis=0).T
                acts.append((two * sg_ref[0, :, pc].astype(F32)).astype(BF16))
        xn = x_ref[0] + mod_ref[0][2:3] * _dot(jnp.concatenate(acts, axis=1), wo_ref[...])
        if final:
            xn = xn * lax.rsqrt(jnp.mean(xn * xn, axis=-1, keepdims=True) + EPS) * fg_ref[...]
        o_ref[0] = xn

    if first_blk >= ctx_blocks:
        heads(True)
    else:
        pl.when(jl >= 0)(lambda: heads(True))
        pl.when(jl < 0)(lambda: heads(False))


def _attn_a(q_t, k, v_t, sg, sink, xc, wo_bf, mod, ctx_len, bsz, final_g=None):
    b, qw, s = q_t.shape
    d = xc.shape[2]
    kw = v_t.shape[1]
    final = final_g is not None
    ctx_blocks = ctx_len // BLOCK
    lat_blocks = (s - ctx_len) // BLOCK
    first_blk = ctx_blocks if final else 0
    nblk = s // BLOCK - first_blk

    def lat(i):
        return jnp.maximum(i + first_blk - ctx_blocks, 0)

    def prev_blk(i):
        return ctx_blocks + jnp.maximum(lat(i) - 1, 0)

    def next_blk(i):
        return ctx_blocks + jnp.minimum(lat(i) + 1, lat_blocks - 1)

    own = lambda bb, i: (bb, i + first_blk, 0)
    qblk = pl.BlockSpec((1, BLOCK, qw), own)
    kblk = lambda f: pl.BlockSpec((1, BLOCK, kw), lambda bb, i: (bb, f(i), 0))
    vblk = lambda f: pl.BlockSpec((1, kw, BLOCK), lambda bb, i: (bb, 0, f(i)))
    body = functools.partial(_attn_a_body, first_blk=first_blk, ctx_blocks=ctx_blocks,
                             lat_blocks=lat_blocks, final=final)
    const2 = lambda bb, i: (0, 0)
    modrow = lambda bb, i: (jnp.where(i + first_blk < ctx_blocks, bsz, bb), 0, 0)
    extra_in, extra_specs = ((final_g,), (pl.BlockSpec((1, d), const2),)) if final else ((), ())
    return pl.pallas_call(
        body,
        out_shape=jax.ShapeDtypeStruct((b, nblk * BLOCK, d), F32),
        grid=(b, nblk),
        in_specs=[pl.BlockSpec(memory_space=pltpu.SMEM),
                  pl.BlockSpec((1, qw, BLOCK), lambda bb, i: (bb, 0, i + first_blk)),
                  kblk(prev_blk), kblk(lambda i: i + first_blk), kblk(next_blk),
                  pl.BlockSpec((1, ctx_len, kw), lambda bb, i: (bb, 0, 0)),
                  vblk(prev_blk), vblk(lambda i: i + first_blk), vblk(next_blk),
                  pl.BlockSpec((1, kw, ctx_len), lambda bb, i: (bb, 0, 0)),
                  qblk,
                  pl.BlockSpec((1, BLOCK, d), own),
                  pl.BlockSpec(wo_bf.shape, const2),
                  pl.BlockSpec((1, 3, d), modrow),
                  *extra_specs],
        out_specs=pl.BlockSpec((1, BLOCK, d), lambda bb, i: (bb, i, 0)),
        scratch_shapes=[pltpu.VMEM((2, ctx_len + 3 * BLOCK, A_GROUP * BLOCK), F32)],
        input_output_aliases={} if final else {11: 0},
        compiler_params=_cparams(("parallel", "arbitrary")),
        name="attn_a",
    )(sink, q_t, k, k, k, k, v_t, v_t, v_t, v_t, sg, xc, wo_bf, mod, *extra_in)


def _proj_b_body(x_ref, mod_ref, g_ref, w_ref, qn_ref, wuqt_ref, kvn_ref, wkn_ref, wvt_ref,
                 cqt_ref, sqt_ref, ck_ref, sk_ref, qt_ref, k_ref, vt_ref, sg_ref, *, q_scale):
    tm = x_ref.shape[1]
    q_lora = qn_ref.shape[1]
    kv_lora = kvn_ref.shape[1]
    h = _norm_mod(x_ref[0], g_ref[...], mod_ref[0]).astype(BF16)
    y = _dot(h, w_ref[...])
    lane = lax.broadcasted_iota(jnp.int32, (tm, LANES), 1)
    even = (((lane - B_NOPE) // (B_ROPE // 4)) % 2 == 0)
    low = lane < B_NOPE

    def rms(u, gain):
        return u * lax.rsqrt(jnp.mean(u * u, axis=-1, keepdims=True) + EPS) * gain

    cq = rms(y[:, :q_lora], qn_ref[...]).astype(BF16)
    ckv = rms(y[:, q_lora:q_lora + kv_lora], kvn_ref[...]).astype(BF16)
    krg = _rope128(y[:, q_lora + kv_lora:q_lora + kv_lora + LANES], ck_ref[...], sk_ref[...],
                   B_ROPE // 4, even)
    qt = _dot_nt(wuqt_ref[...], cq)
    cqt, sqt = cqt_ref[...], sqt_ref[...]
    for hd in range(B_HEADS):
        r0 = hd * LANES
        qt_ref[0, r0:r0 + B_NOPE, :] = (qt[r0:r0 + B_NOPE] * q_scale).astype(BF16)
        qt_ref[0, r0 + B_NOPE:r0 + B_NOPE + B_ROPE, :] = _rope_rows(
            qt[r0 + B_NOPE:r0 + B_NOPE + B_ROPE], cqt, sqt).astype(BF16)
        qt_ref[0, r0 + B_NOPE + B_ROPE:r0 + LANES, :] = qt[r0 + B_NOPE + B_ROPE:r0 + LANES].astype(BF16)
    vt_ref[0] = _dot_nt(wvt_ref[...], ckv).astype(BF16)
    kn = _dot(ckv, wkn_ref[...])
    for hd in range(B_HEADS):
        grp = kn[:, (hd // 2) * LANES:(hd // 2 + 1) * LANES]
        if hd % 2 == 1:
            grp = pltpu.roll(grp, B_NOPE, 1)
        k_ref[0, :, hd * LANES:(hd + 1) * LANES] = jnp.where(low, grp, krg).astype(BF16)
    sg_ref[0] = _silu(y[:, q_lora + kv_lora + LANES:]).astype(BF16)


def _proj_b(xc, mod, g, w_bf, qn, wuqt_bf, kvn, wkn_bf, wvt_bf, tabs, q_scale, n_ctx_tiles, bsz):
    b, s, d = xc.shape
    tm = ROW_TILE
    hw = B_HEADS * LANES
    gw = B_HEADS * B_V
    row = lambda bb, i: (bb, i, 0)
    modrow = lambda bb, i: (jnp.where(i < n_ctx_tiles, bsz, bb), 0, 0)
    const2 = lambda bb, i: (0, 0)
    tab = pl.BlockSpec((tm, LANES), lambda bb, i: (i, 0))
    tab_t = pl.BlockSpec((B_ROPE, tm), lambda bb, i: (0, i))
    big = jax.ShapeDtypeStruct((b, s, hw), BF16)
    full = lambda a: pl.BlockSpec(a.shape, const2)
    return pl.pallas_call(
        functools.partial(_proj_b_body, q_scale=q_scale),
        out_shape=(jax.ShapeDtypeStruct((b, hw, s), BF16), big, jax.ShapeDtypeStruct((b, gw, s), BF16), jax.ShapeDtypeStruct((b, s, gw), BF16)),
        grid=(b, s // tm),
        in_specs=[pl.BlockSpec((1, tm, d), row),
                  pl.BlockSpec((1, 3, d), modrow),
                  pl.BlockSpec((1, d), const2),
                  full(w_bf), full(qn), full(wuqt_bf), full(kvn), full(wkn_bf), full(wvt_bf),
                  tab_t, tab_t, tab, tab],
        out_specs=(pl.BlockSpec((1, hw, tm), lambda bb, i: (bb, 0, i)), pl.BlockSpec((1, tm, hw), row),
                   pl.BlockSpec((1, gw, tm), lambda bb, i: (bb, 0, i)), pl.BlockSpec((1, tm, gw), row)),
        compiler_params=_cparams(("parallel", "arbitrary")),
        name="proj_b",
    )(xc, mod, g, w_bf, qn, wuqt_bf, kvn, wkn_bf, wvt_bf, *tabs)


B_KEY_CHUNK = 256


def _attn_b_body(qt_ref, k_ref, vt_ref, sg_ref, x_ref, wo_ref, mod_ref, o_ref, s_ref,
                 *, ctx_len, n_ctx_tiles):
    i = pl.program_id(1)
    nh = B_HEADS
    kc = B_KEY_CHUNK

    def attend(nk):
        nc = nk // kc

        def scores(hd, c):
            sl = slice(hd * LANES, (hd + 1) * LANES)
            s_t = _dot(k_ref[0, c * kc:(c + 1) * kc, sl], qt_ref[0, sl, :])
            s_ref[hd % 2, c * kc:(c + 1) * kc, :] = s_t
            return jnp.max(s_t, axis=0, keepdims=True)

        def weighted(hd, c, m):
            p = jnp.exp2(s_ref[hd % 2, c * kc:(c + 1) * kc, :] - m).astype(BF16)
            v1 = _with_ones(vt_ref[0, hd * B_V:(hd + 1) * B_V, c * kc:(c + 1) * kc])
            return _dot(v1, p)

        outs = []
        mx = [scores(0, c) for c in range(nc)]
        for hd in range(nh):
            m = functools.reduce(jnp.maximum, mx)
            mx = []
            ov = None
            for c in range(nc):
                part = weighted(hd, c, m)
                ov = part if ov is None else ov + part
                if hd + 1 < nh:
                    mx.append(scores(hd + 1, c))
            outs.append(ov[:B_V] * (1.0 / ov[B_V:B_V + 1]))
        o_t = jnp.concatenate(outs, axis=0)
        act = (o_t.T * sg_ref[0].astype(F32)).astype(BF16)
        o_ref[0] = x_ref[0] + mod_ref[0][2:3] * _dot(act, wo_ref[...])

    pl.when(i < n_ctx_tiles)(lambda: attend(ctx_len))
    pl.when(i >= n_ctx_tiles)(lambda: attend(k_ref.shape[1]))


def _attn_b(q_t, kpad, v_t, sg, xc, wo_bf, mod, ctx_len, bsz):
    b, s, hw = kpad.shape
    d = xc.shape[2]
    tq = ROW_TILE
    n_ctx_tiles = ctx_len // tq
    body = functools.partial(_attn_b_body, ctx_len=ctx_len, n_ctx_tiles=n_ctx_tiles)
    row = lambda bb, i: (bb, i, 0)
    whole = lambda bb, i: (bb, 0, 0)
    return pl.pallas_call(
        body,
        out_shape=jax.ShapeDtypeStruct(xc.shape, F32),
        grid=(b, s // tq),
        in_specs=[pl.BlockSpec((1, hw, tq), lambda bb, i: (bb, 0, i)),
                  pl.BlockSpec((1, s, hw), whole),
                  pl.BlockSpec((1, v_t.shape[1], s), whole),
                  pl.BlockSpec((1, tq, sg.shape[2]), row),
                  pl.BlockSpec((1, tq, d), row),
                  pl.BlockSpec(wo_bf.shape, lambda bb, i: (0, 0)),
                  pl.BlockSpec((1, 3, d), lambda bb, i: (jnp.where(i < n_ctx_tiles, bsz, bb), 0, 0))],
        out_specs=pl.BlockSpec((1, tq, d), row),
        scratch_shapes=[pltpu.VMEM((2, s, tq), F32)],
        input_output_aliases={4: 0},
        compiler_params=_cparams(("parallel", "arbitrary")),
        name="attn_b",
    )(q_t, kpad, v_t, sg, xc, wo_bf, mod)


C_ROW_TILE = 256


def _proj_c_body(x_ref, modc_ref, modx_ref, g_ref, wdt_ref, dtb_ref, w_ref, cw_ref, cb_ref,
                 z_ref, xbc_ref, dt_ref, ht_ref, *, ctx_len, z_tiles):
    j = pl.program_id(1)
    s = x_ref.shape[1]
    tm = ROW_TILE

    @pl.when(j == 0)
    def _():
        for r in range(s // tm):
            mod = modc_ref[0] if r * tm < ctx_len else modx_ref[0]
            hh = _norm_mod(x_ref[0, r * tm:(r + 1) * tm, :], g_ref[...], mod)
            ht_ref[:, r * tm:(r + 1) * tm] = hh.T.astype(BF16)
        dt = _dot(wdt_ref[...], ht_ref[...]) + dtb_ref[...]
        dt_ref[0] = jnp.maximum(dt, 0.0) + jnp.log1p(jnp.exp(-jnp.abs(dt)))

    @pl.when(j < z_tiles)
    def _():
        z_ref[0] = _dot(w_ref[...], ht_ref[...]).astype(BF16)

    @pl.when(j >= z_tiles)
    def _():
        u = _dot(w_ref[...], ht_ref[...])
        t = lax.broadcasted_iota(jnp.int32, (1, s), 1)
        has_prev = (t != 0) & (t != ctx_len)
        has_next = (t != ctx_len - 1) & (t != s - 1)
        up = jnp.where(has_prev, pltpu.roll(u, 1, 1), 0.0)
        un = jnp.where(has_next, pltpu.roll(u, s - 1, 1), 0.0)
        cw = cw_ref[...]
        v = cw[:, 0:1] * up + cw[:, 1:2] * u + cw[:, 2:3] * un + cb_ref[...]
        xbc_ref[0] = _silu(v).astype(BF16)


def _proj_c(xc, mod, g, wt_bf, wdt_bf, dtb, cwt, cbt, ctx_len, bsz, inner):
    b, s, d = xc.shape
    tc = C_ROW_TILE
    nrows = wt_bf.shape[0]
    z_tiles = inner // tc
    n_tiles = nrows // tc
    conv_dim = nrows - inner
    ndt = wdt_bf.shape[0]
    body = functools.partial(_proj_c_body, ctx_len=ctx_len, z_tiles=z_tiles)
    const2 = lambda bb, j: (0, 0)
    return pl.pallas_call(
        body,
        out_shape=(jax.ShapeDtypeStruct((b, inner, s), BF16),
                   jax.ShapeDtypeStruct((b, conv_dim, s), BF16),
                   jax.ShapeDtypeStruct((b, ndt, s), F32)),
        grid=(b, n_tiles),
        in_specs=[pl.BlockSpec((1, s, d), lambda bb, j: (bb, 0, 0)),
                  pl.BlockSpec((1, 3, d), lambda bb, j: (bsz, 0, 0)),
                  pl.BlockSpec((1, 3, d), lambda bb, j: (bb, 0, 0)),
                  pl.BlockSpec((1, d), const2),
                  pl.BlockSpec((ndt, d), const2),
                  pl.BlockSpec((ndt, 1), const2),
                  pl.BlockSpec((tc, d), lambda bb, j: (j, 0)),
                  pl.BlockSpec((tc, 3), lambda bb, j: (jnp.maximum(j - z_tiles, 0), 0)),
                  pl.BlockSpec((tc, 1), lambda bb, j: (jnp.maximum(j - z_tiles, 0), 0))],
        out_specs=(pl.BlockSpec((1, tc, s), lambda bb, j: (bb, jnp.minimum(j, z_tiles - 1), 0)),
                   pl.BlockSpec((1, tc, s), lambda bb, j: (bb, jnp.maximum(j - z_tiles, 0), 0)),
                   pl.BlockSpec((1, ndt, s), lambda bb, j: (bb, 0, 0))),
        scratch_shapes=[pltpu.VMEM((d, s), BF16)],
        compiler_params=_cparams(("parallel", "arbitrary")),
        name="proj_c",
    )(xc, mod, mod, g, wdt_bf, dtb, wt_bf, cwt, cbt)


def _ssd_chunk(t, n_chunks, ctx_chunks):
    tb = t - n_chunks
    cb = jnp.where(tb < ctx_chunks, ctx_chunks - 1 - tb, n_chunks - 1 - (tb - ctx_chunks))
    return jnp.where(t < n_chunks, t, cb)


def _rep_rows(a, reps):
    r = a.shape[0]
    return jnp.broadcast_to(a[:, None, :], (r, reps, a.shape[1])).reshape(r * reps, a.shape[1])


def _ssd_body(z_ref, xbc_ref, dt_ref, dtn_ref, a_ref, dsk_ref, nw_ref, x_ref, wo_ref, mod_ref,
              o_ref, yf_ref, st_ref, yb_ref, pro_ref, *, n_chunks, ctx_chunks, heads, inner):
    t = pl.program_id(1)
    q = C_CHUNK
    hpg = heads // C_GROUPS
    gw = hpg * C_HEAD_DIM
    bwd = t >= n_chunks
    d = bwd.astype(jnp.int32)
    c = _ssd_chunk(t, n_chunks, ctx_chunks)
    ii = lax.broadcasted_iota(jnp.int32, (q, q), 0)
    jj = lax.broadcasted_iota(jnp.int32, (q, q), 1)

    def decay_terms(dt_blk_ref, step):
        rev = step >= n_chunks
        rows = pl.ds(pl.multiple_of(rev.astype(jnp.int32) * heads, heads), heads)
        dtd = dt_blk_ref[0, rows, :]
        a = dtd * a_ref[rows, :]
        tri = (ii <= jj).astype(BF16)
        a_hi, a_lo = _split2(a)
        a_lo2 = (a - a_hi.astype(F32) - a_lo.astype(F32)).astype(BF16)
        cum = _dot(a_hi, tri) + _dot(a_lo, tri) + _dot(a_lo2, tri)
        tot = jnp.broadcast_to(cum[:, q - 1:q], cum.shape)
        u = jnp.where(rev, tot - cum + a, cum)
        pro_ref[0] = dtd
        pro_ref[1] = u * LOG2E
        pro_ref[2] = jnp.exp(u)
        pro_ref[3] = dtd * jnp.exp(tot - u)
        pro_ref[4] = jnp.exp(tot)

    @pl.when(t == 0)
    def _():
        decay_terms(dt_ref, t)

    @pl.when((t == 0) | (t == n_chunks))
    def _():
        st_ref[...] = jnp.zeros_like(st_ref)

    dtd, u2, eu, dtdend, etot = (pro_ref[i] for i in range(5))
    decay_terms(dtn_ref, t + 1)
    sgn = 1 - 2 * d
    keep = (jj - ii) * sgn >= 0

    for g in range(C_GROUPS):
        r0 = g * gw
        hs = slice(g * hpg, (g + 1) * hpg)
        b_t = xbc_ref[0, inner + g * C_STATE: inner + (g + 1) * C_STATE, :]
        c_t = xbc_ref[0, inner + (C_GROUPS + g) * C_STATE: inner + (C_GROUPS + g + 1) * C_STATE, :]
        b_g = b_t.astype(F32).T.astype(BF16)
        cb_t = _dot(b_g, c_t)
        xs = xbc_ref[0, r0:r0 + gw, :].astype(F32)
        x_bf = (xs * _rep_rows(dtd[hs], C_HEAD_DIM)).astype(BF16)
        st = st_ref[r0:r0 + gw, :]
        y_off = _dot(st.astype(BF16), c_t) * _rep_rows(eu[hs], C_HEAD_DIM)
        st_ref[r0:r0 + gw, :] = (_rep_rows(etot[hs], C_HEAD_DIM) * st
                                 + _dot((xs * _rep_rows(dtdend[hs], C_HEAD_DIM)).astype(BF16), b_g))
        for hh in range(hpg):
            hd = g * hpg + hh
            r = jnp.broadcast_to(u2[hd:hd + 1, :], (q, q))
            e = jnp.where(keep, r - r.T, NEG)
            m_t = (jnp.exp2(e) * cb_t).astype(BF16)
            rr = slice(hh * C_HEAD_DIM, (hh + 1) * C_HEAD_DIM)
            yb_ref[r0 + hh * C_HEAD_DIM:r0 + (hh + 1) * C_HEAD_DIM, :] = _dot(x_bf[rr], m_t) + y_off[rr]

    @pl.when(jnp.logical_not(bwd))
    def _():
        yf_ref[c] = yb_ref[...] + dsk_ref[...] * xbc_ref[0, 0:inner, :].astype(F32)

    @pl.when(bwd)
    def _():
        acts = []
        for g in range(C_GROUPS):
            r0 = g * gw
            rs = slice(r0, r0 + gw)
            y = yf_ref[c, rs, :] + yb_ref[rs, :]
            y = y * _silu(z_ref[0, rs, :].astype(F32))
            ms = jnp.mean(y * y, axis=0, keepdims=True)
            y = y * lax.rsqrt(ms + EPS) * nw_ref[rs, :]
            acts.append(y.T.astype(BF16))
        o_ref[0] = x_ref[0] + mod_ref[0][2:3] * _dot(jnp.concatenate(acts, axis=1), wo_ref[...])


def _ssd(z_t, xbc_t, dt_t, a_tab, dsk_tab, nw_tab, xc, wo_bf, mod, ctx_len, bsz):
    b, inner, s = z_t.shape
    d = xc.shape[2]
    conv_dim = xbc_t.shape[1]
    heads = inner // C_HEAD_DIM
    n_chunks = s // C_CHUNK
    ctx_chunks = ctx_len // C_CHUNK
    body = functools.partial(_ssd_body, n_chunks=n_chunks, ctx_chunks=ctx_chunks, heads=heads,
                             inner=inner)
    cmap = lambda bb, t: (bb, 0, _ssd_chunk(t, n_chunks, ctx_chunks))
    nmap = lambda bb, t: (bb, 0, _ssd_chunk(jnp.minimum(t + 1, 2 * n_chunks - 1), n_chunks, ctx_chunks))
    out_chunk = lambda t: _ssd_chunk(jnp.maximum(t, n_chunks), n_chunks, ctx_chunks)
    omap = lambda bb, t: (bb, out_chunk(t), 0)
    modrow = lambda bb, t: (jnp.where(out_chunk(t) < ctx_chunks, bsz, bb), 0, 0)
    const2 = lambda bb, t: (0, 0)
    return pl.pallas_call(
        body,
        out_shape=jax.ShapeDtypeStruct(xc.shape, F32),
        grid=(b, 2 * n_chunks),
        in_specs=[pl.BlockSpec((1, inner, C_CHUNK), cmap),
                  pl.BlockSpec((1, conv_dim, C_CHUNK), cmap),
                  pl.BlockSpec((1, 2 * heads, C_CHUNK), cmap),
                  pl.BlockSpec((1, 2 * heads, C_CHUNK), nmap),
                  pl.BlockSpec((2 * heads, LANES), const2),
                  pl.BlockSpec((inner, LANES), const2),
                  pl.BlockSpec((inner, LANES), const2),
                  pl.BlockSpec((1, C_CHUNK, d), omap),
                  pl.BlockSpec(wo_bf.shape, const2),
                  pl.BlockSpec((1, 3, d), modrow)],
        out_specs=pl.BlockSpec((1, C_CHUNK, d), omap),
        scratch_shapes=[pltpu.VMEM((n_chunks, inner, C_CHUNK), F32),
                        pltpu.VMEM((inner, C_STATE), F32),
                        pltpu.VMEM((inner, C_CHUNK), F32),
                        pltpu.VMEM((5, heads, C_CHUNK), F32)],
        input_output_aliases={7: 0},
        compiler_params=_cparams(("parallel", "arbitrary")),
        name="ssd_scan",
    )(z_t, xbc_t, dt_t, dt_t, a_tab, dsk_tab, nw_tab, xc, wo_bf, mod)


def _lanes(v):
    return jnp.broadcast_to(v.astype(F32)[:, None], (v.shape[0], LANES))


def kernel(x, c, ctx, c_ctx, ada_w, ada_b, norm_g, final_g, a_w_in, a_sink, a_w_out, b_w_in, b_q_norm, b_w_uq, b_kv_norm, b_w_ukv, b_w_out, c_w_in, c_conv_w, c_conv_b, c_dt_bias, c_a_log, c_d, c_norm, c_w_out):
    bsz, t_len, d = x.shape
    ctx_len = ctx.shape[1]
    depth = ada_w.shape[0]
    assert ctx_len % ROW_TILE == 0 and t_len % ROW_TILE == 0 and t_len % GRID_W == 0
    n_ctx_tiles = ctx_len // ROW_TILE
    rows = t_len // GRID_W

    xc = jnp.concatenate([ctx, x], axis=1)

    r_pad = -(-(bsz + 1) // 8) * 8
    cond = jnp.concatenate([c, c_ctx[None], jnp.zeros((r_pad - bsz - 1, d), F32)], axis=0)
    mods = _adaln_all(cond, ada_w, ada_b).reshape(depth, r_pad, 3, d)

    lane = jnp.arange(LANES)
    a_scale = A_HEAD_DIM ** -0.5 * LOG2E
    a_lane_dim = lane % A_HEAD_DIM
    a_tabs = (_rope_tables_t(rows, A_HEAD_DIM, ctx_len, a_scale)
              + _rope_tables(rows, A_HEAD_DIM, ctx_len, a_lane_dim, 1.0))
    b_scale = (B_NOPE + B_ROPE) ** -0.5 * LOG2E
    b_lane_dim = jnp.where((lane >= B_NOPE) & (lane < B_NOPE + B_ROPE), lane - B_NOPE, -1)
    b_tabs = (_rope_tables_t(rows, B_ROPE, ctx_len, b_scale)
              + _rope_tables(rows, B_ROPE, ctx_len, b_lane_dim, 1.0))

    for i in range(depth):
        kind = i % N_MIXERS
        j = i // N_MIXERS
        last = i == depth - 1
        mod = mods[i]
        g = norm_g[i][None]
        if kind == 0:
            w = a_w_in[j]
            qw, kw = A_HEADS * A_HEAD_DIM, A_KV_HEADS * A_HEAD_DIM
            w_tok = jnp.concatenate([w[:, qw:qw + kw], w[:, qw + 2 * kw:]], axis=1)
            q_t, k, v_t, sg = _proj_a(xc, mod, g, w[:, :qw].T.astype(BF16),
                                      w[:, qw + kw:qw + 2 * kw].T.astype(BF16), w_tok.astype(BF16),
                                      a_tabs, n_ctx_tiles, bsz)
            xc = _attn_a(q_t, k, v_t, sg, a_sink[j].astype(F32), xc, a_w_out[j].astype(BF16), mod,
                         ctx_len, bsz, final_g=final_g[None] if last else None)
        elif kind == 1:
            w = b_w_in[j]
            q_lora = b_q_norm.shape[1]
            kv_lora = b_kv_norm.shape[1]
            k0 = q_lora + kv_lora
            w_pad = jnp.concatenate(
                [w[:, :k0], jnp.zeros((d, B_NOPE), F32), w[:, k0:k0 + B_ROPE],
                 jnp.zeros((d, LANES - B_NOPE - B_ROPE), F32), w[:, k0 + B_ROPE:]], axis=1)
            wuq = b_w_uq[j].reshape(q_lora, B_HEADS, B_NOPE + B_ROPE)
            wuq = jnp.pad(wuq, ((0, 0), (0, 0), (0, LANES - B_NOPE - B_ROPE))).reshape(q_lora, B_HEADS * LANES)
            wukv = b_w_ukv[j].reshape(kv_lora, B_HEADS, B_NOPE + B_V)
            wkn = wukv[:, :, :B_NOPE].reshape(kv_lora, B_HEADS * B_NOPE)
            wv = wukv[:, :, B_NOPE:].reshape(kv_lora, B_HEADS * B_V)
            q_t, kp, v_t, sg = _proj_b(xc, mod, g, w_pad.astype(BF16), b_q_norm[j][None],
                                       wuq.T.astype(BF16), b_kv_norm[j][None], wkn.astype(BF16),
                                       wv.T.astype(BF16), b_tabs, b_scale, n_ctx_tiles, bsz)
            xc = _attn_b(q_t, kp, v_t, sg, xc, b_w_out[j].astype(BF16), mod, ctx_len, bsz)
        else:
            inner = c_norm.shape[1]
            heads = inner // C_HEAD_DIM
            conv_dim = c_conv_w.shape[2]
            w = c_w_in[j]
            wt = w[:, :inner + conv_dim].T.astype(BF16)
            wdt = w[:, inner + conv_dim:].T.astype(BF16)
            z_t, xbc_t, dt_t = _proj_c(xc, mod, g, wt, wdt, c_dt_bias[j].reshape(2 * heads, 1),
                                       c_conv_w[j].T, c_conv_b[j][:, None], ctx_len, bsz, inner)
            a_tab = _lanes(-jnp.exp(c_a_log[j].astype(F32)).reshape(2 * heads))
            dsk_tab = _lanes(jnp.repeat(c_d[j], C_HEAD_DIM))
            xc = _ssd(z_t, xbc_t, dt_t, a_tab, dsk_tab, _lanes(c_norm[j]), xc, c_w_out[j].astype(BF16), mod,
                      ctx_len, bsz)
    assert (depth - 1) % N_MIXERS == 0
    return xc
```

```python
import functools
import math

import jax
import jax.numpy as jnp
from jax import lax
from jax.experimental import pallas as pl
from jax.experimental.pallas import tpu as pltpu

F32 = jnp.float32
BF16 = jnp.bfloat16

GRID_W = 64
N_MIXERS = 3
ROPE_BASE = 10000.0
EPS = 1e-6
BLOCK = 128
WINDOW = 128

A_HEADS = 16
A_KV_HEADS = 4
A_GROUP = A_HEADS // A_KV_HEADS
A_HEAD_DIM = 64

B_HEADS = 16
B_NOPE = 64
B_ROPE = 32
B_V = 64

C_HEAD_DIM = 64
C_GROUPS = 4
C_STATE = 128
C_CHUNK = 128

LANES = 128
BF16_ROWS = 16
ROW_TILE = 256
VMEM_LIMIT = 56 * 1024 * 1024
NEG = -1e30
LOG2E = math.log2(math.e)


def _cparams(sem):
    return pltpu.CompilerParams(dimension_semantics=sem, vmem_limit_bytes=VMEM_LIMIT)


def _dot(a, b):
    return jnp.dot(a, b, preferred_element_type=F32)


def _dot_nt(a, b):
    return lax.dot_general(a, b, (((1,), (1,)), ((), ())), preferred_element_type=F32)


def _silu(v):
    return v / (1.0 + jnp.exp(-v))


def _split2(a):
    hi = a.astype(BF16)
    lo = (a - hi.astype(F32)).astype(BF16)
    return hi, lo


def _dot_f32(a, b):
    a_hi, a_lo = _split2(a)
    b_hi, b_lo = _split2(b)
    return _dot(a_hi, b_hi) + _dot(a_hi, b_lo) + _dot(a_lo, b_hi)


def _norm_mod(x, g, mod):
    ms = jnp.mean(x * x, axis=-1, keepdims=True)
    y = x * lax.rsqrt(ms + EPS) * g
    return y * (1.0 + mod[1:2]) + mod[0:1]


def _with_ones(v_t):
    return jnp.concatenate([v_t, jnp.ones((BF16_ROWS, v_t.shape[1]), BF16)], axis=0)


def _adaln_body(cond_ref, w_ref, b_ref, o_ref):
    o_ref[0] = _dot_f32(_silu(cond_ref[...]), w_ref[0]) + b_ref[0]


def _adaln_all(cond, ada_w, ada_b):
    depth, d, _ = ada_w.shape
    r = cond.shape[0]
    return pl.pallas_call(
        _adaln_body,
        out_shape=jax.ShapeDtypeStruct((depth, r, 3 * d), F32),
        grid=(depth, 3),
        in_specs=[pl.BlockSpec((r, d), lambda i, j: (0, 0)),
                  pl.BlockSpec((1, d, d), lambda i, j: (i, 0, j)),
                  pl.BlockSpec((1, 1, d), lambda i, j: (i, 0, j))],
        out_specs=pl.BlockSpec((1, r, d), lambda i, j: (i, 0, j)),
        compiler_params=_cparams(("arbitrary", "arbitrary")),
        name="adaln_mod",
    )(cond, ada_w, ada_b.reshape(depth, 1, 3 * d))


def _rope_tables(rows, dim, ctx_len, lane_of_dim, scale):
    row = jnp.repeat(jnp.arange(rows), GRID_W).astype(F32)
    col = (jnp.arange(rows * GRID_W) % GRID_W).astype(F32)
    nf = dim // 4
    inv = ROPE_BASE ** (-jnp.arange(nf, dtype=F32) / nf)
    ar = row[:, None] * inv
    ac = col[:, None] * inv
    ang = jnp.concatenate([ar, ar, ac, ac], axis=-1)
    cos, sin = jnp.cos(ang), jnp.sin(ang)
    sign = jnp.where((jnp.arange(dim) // nf) % 2 == 0, -1.0, 1.0).astype(F32)
    sin = sin * sign
    idx = jnp.asarray(lane_of_dim)
    valid = (idx >= 0)[None, :]
    cos_l = jnp.where(valid, cos[:, jnp.maximum(idx, 0)], 1.0)
    sin_l = jnp.where(valid, sin[:, jnp.maximum(idx, 0)], 0.0)
    cos_l = jnp.concatenate([jnp.ones((ctx_len, LANES), F32), cos_l], axis=0)
    sin_l = jnp.concatenate([jnp.zeros((ctx_len, LANES), F32), sin_l], axis=0)
    return cos_l * scale, sin_l * scale


def _rope128(u, cos, sin, quarter, even):
    fwd = pltpu.roll(u, quarter, 1)
    bwd = pltpu.roll(u, LANES - quarter, 1)
    return u * cos + jnp.where(even, bwd, fwd) * sin


def _rope_tables_t(rows, dim, ctx_len, scale):
    row = jnp.repeat(jnp.arange(rows), GRID_W).astype(F32)
    col = (jnp.arange(rows * GRID_W) % GRID_W).astype(F32)
    nf = dim // 4
    inv = ROPE_BASE ** (-jnp.arange(nf, dtype=F32) / nf)
    ar = inv[:, None] * row[None, :]
    ac = inv[:, None] * col[None, :]
    ang = jnp.concatenate([ar, ar, ac, ac], axis=0)
    cos = jnp.concatenate([jnp.ones((dim, ctx_len), F32), jnp.cos(ang)], axis=1)
    sin = jnp.concatenate([jnp.zeros((dim, ctx_len), F32), jnp.sin(ang)], axis=1)
    return cos * scale, sin * scale


def _rope_rows(u, cos_t, sin_t):
    qd = u.shape[0] // 4
    r1, r2, c1, c2 = (u[i * qd:(i + 1) * qd] for i in range(4))
    rot = jnp.concatenate([-r2, r1, -c2, c1], axis=0)
    return u * cos_t + rot * sin_t


def _proj_a_body(x_ref, mod_ref, g_ref, wqt_ref, wvt_ref, w_ref, cqt_ref, sqt_ref, ck_ref, sk_ref,
                 qt_ref, k_ref, vt_ref, sg_ref):
    tm = x_ref.shape[1]
    h = _norm_mod(x_ref[0], g_ref[...], mod_ref[0]).astype(BF16)
    kw = A_KV_HEADS * A_HEAD_DIM
    qt = _dot_nt(wqt_ref[...], h)
    cqt, sqt = cqt_ref[...], sqt_ref[...]
    for hd in range(A_HEADS):
        rs = slice(hd * A_HEAD_DIM, (hd + 1) * A_HEAD_DIM)
        qt_ref[0, rs, :] = _rope_rows(qt[rs], cqt, sqt).astype(BF16)
    vt_ref[0] = _dot_nt(wvt_ref[...], h).astype(BF16)
    y = _dot(h, w_ref[...])
    lane = lax.broadcasted_iota(jnp.int32, (tm, LANES), 1)
    even = ((lane % A_HEAD_DIM) // (A_HEAD_DIM // 4)) % 2 == 0
    ck, sk = ck_ref[...], sk_ref[...]
    for j in range(kw // LANES):
        kg = _rope128(y[:, j * LANES:(j + 1) * LANES], ck, sk, A_HEAD_DIM // 4, even)
        k_ref[0, :, j * LANES:(j + 1) * LANES] = kg.astype(BF16)
    sg_ref[0] = _silu(y[:, kw:]).astype(BF16)


def _proj_a(xc, mod, g, wqt_bf, wvt_bf, w_bf, tabs, n_ctx_tiles, bsz):
    b, s, d = xc.shape
    tm = ROW_TILE
    qw = A_HEADS * A_HEAD_DIM
    kw = A_KV_HEADS * A_HEAD_DIM
    row = lambda bb, i: (bb, i, 0)
    modrow = lambda bb, i: (jnp.where(i < n_ctx_tiles, bsz, bb), 0, 0)
    const2 = lambda bb, i: (0, 0)
    full = lambda a: pl.BlockSpec(a.shape, const2)
    tab = pl.BlockSpec((tm, LANES), lambda bb, i: (i, 0))
    tab_t = pl.BlockSpec((A_HEAD_DIM, tm), lambda bb, i: (0, i))
    wide = jax.ShapeDtypeStruct((b, s, qw), BF16)
    return pl.pallas_call(
        _proj_a_body,
        out_shape=(jax.ShapeDtypeStruct((b, qw, s), BF16), jax.ShapeDtypeStruct((b, s, kw), BF16),
                   jax.ShapeDtypeStruct((b, kw, s), BF16), wide),
        grid=(b, s // tm),
        in_specs=[pl.BlockSpec((1, tm, d), row),
                  pl.BlockSpec((1, 3, d), modrow),
                  pl.BlockSpec((1, d), const2),
                  full(wqt_bf), full(wvt_bf), full(w_bf),
                  tab_t, tab_t, tab, tab],
        out_specs=(pl.BlockSpec((1, qw, tm), lambda bb, i: (bb, 0, i)), pl.BlockSpec((1, tm, kw), row),
                   pl.BlockSpec((1, kw, tm), lambda bb, i: (bb, 0, i)), pl.BlockSpec((1, tm, qw), row)),
        compiler_params=_cparams(("parallel", "arbitrary")),
        name="proj_a",
    )(xc, mod, g, wqt_bf, wvt_bf, w_bf, *tabs)


A_BLOCKS_PER_STEP = 2


def _attn_a_body(sink_ref, qt_ref, kp_ref, ko_ref, kn_ref, kc_ref, vp_ref, vo_ref, vn_ref, vc_ref,
                 sg_ref, x_ref, wo_ref, mod_ref, *rest, first_pair, ctx_pairs, lat_pairs, final):
    if final:
        fg_ref, o_ref, s_ref = rest
    else:
        o_ref, s_ref = rest
    jp = pl.program_id(1) + first_pair - ctx_pairs
    nq = BLOCK
    nsub = A_BLOCKS_PER_STEP
    cols = A_GROUP * nq
    colq = lax.broadcasted_iota(jnp.int32, (1, cols), 1)
    nctx = kc_ref.shape[1]

    def run(local):
        if local:
            kk = lax.broadcasted_iota(jnp.int32, (nq, cols), 0)
            qi = lax.broadcasted_iota(jnp.int32, (nq, cols), 1) % nq
            far = 4 * nq
            no_prev = jnp.where(jp > 0, 0, far)
            no_next = jnp.where(jp < lat_pairs - 1, 0, far)
            every = jnp.full((nq, cols), True)
            ok = [jnp.concatenate([kk - qi >= no_prev, every, kk <= qi], axis=0),
                  jnp.concatenate([kk >= qi, every, qi - kk >= no_next], axis=0)]

        def k_local(sub, hl):
            own = [ko_ref[0, 0:nq, hl], ko_ref[0, nq:2 * nq, hl]]
            return [kp_ref[0, :, hl]] + own if sub == 0 else own + [kn_ref[0, :, hl]]

        def v_local(sub, hr):
            own = [vo_ref[0, hr, 0:nq], vo_ref[0, hr, nq:2 * nq]]
            return [vp_ref[0, hr, :]] + own if sub == 0 else own + [vn_ref[0, hr, :]]

        units = [(sub, h) for sub in range(nsub) for h in range(A_KV_HEADS)]

        def sink_row(h):
            row = jnp.zeros((1, cols), F32)
            for g in range(A_GROUP):
                row = jnp.where(colq // nq == g, sink_ref[h * A_GROUP + g], row)
            return row * LOG2E

        def scores(u, part):
            sub, h = units[u]
            q4 = jnp.concatenate(
                [qt_ref[0, (h * A_GROUP + g) * A_HEAD_DIM:(h * A_GROUP + g + 1) * A_HEAD_DIM, sub * nq:(sub + 1) * nq]
                 for g in range(A_GROUP)], axis=1)
            zero = jnp.zeros_like(q4)
            rhs = jnp.concatenate([q4, zero] if h % 2 == 0 else [zero, q4], axis=0)
            hl = slice((h // 2) * LANES, (h // 2 + 1) * LANES)
            if part == 0:
                s_t = _dot(kc_ref[0, :, hl], rhs)
                s_ref[u % 2, 0:nctx, :] = s_t
            else:
                s_t = jnp.where(ok[sub], _dot(jnp.concatenate(k_local(sub, hl), axis=0), rhs), NEG)
                s_ref[u % 2, nctx:nctx + 3 * nq, :] = s_t
            return jnp.max(s_t, axis=0, keepdims=True)

        def weighted(u, part, m):
            sub, h = units[u]
            hr = slice(h * A_HEAD_DIM, (h + 1) * A_HEAD_DIM)
            if part == 0:
                v1 = _with_ones(vc_ref[0, hr, :])
                p = jnp.exp2(s_ref[u % 2, 0:nctx, :] - m)
            else:
                v1 = _with_ones(jnp.concatenate(v_local(sub, hr), axis=1))
                p = jnp.exp2(s_ref[u % 2, nctx:nctx + 3 * nq, :] - m)
            return _dot(v1, p.astype(BF16))

        parts = (0, 1) if local else (0,)
        acts = [[] for _ in range(nsub)]
        mx = [scores(0, part) for part in parts]
        for u, (sub, h) in enumerate(units):
            snk = sink_row(h)
            m = functools.reduce(jnp.maximum, mx + [snk])
            mx = []
            ov = None
            for part in parts:
                piece = weighted(u, part, m)
                ov = piece if ov is None else ov + piece
                if u + 1 < len(units):
                    mx.append(scores(u + 1, part))
            den = ov[A_HEAD_DIM:A_HEAD_DIM + 1] + jnp.exp2(snk - m)
            out = ov[:A_HEAD_DIM] * (1.0 / den)
            c0 = h * A_GROUP * A_HEAD_DIM
            for pair in range(A_GROUP // 2):
                pc = slice(c0 + pair * LANES, c0 + (pair + 1) * LANES)
                two = jnp.concatenate([out[:, (2 * pair) * nq:(2 * pair + 1) * nq],
                                       out[:, (2 * pair + 1) * nq:(2 * pair + 2) * nq]], axis=0).T
                gate = sg_ref[0, sub * nq:(sub + 1) * nq, pc].astype(F32)
                acts[sub].append((two * gate).astype(BF16))
        act = jnp.concatenate([jnp.concatenate(a, axis=1) for a in acts], axis=0)
        xn = x_ref[0] + mod_ref[0][2:3] * _dot(act, wo_ref[...])
        if final:
            xn = xn * lax.rsqrt(jnp.mean(xn * xn, axis=-1, keepdims=True) + EPS) * fg_ref[...]
        o_ref[0] = xn

    if first_pair >= ctx_pairs:
        run(True)
    else:
        pl.when(jp >= 0)(lambda: run(True))
        pl.when(jp < 0)(lambda: run(False))


def _attn_a(q_t, k, v_t, sg, sink, xc, wo_bf, mod, ctx_len, bsz, final_g=None):
    b, qw, s = q_t.shape
    d = xc.shape[2]
    kw = v_t.shape[1]
    final = final_g is not None
    rows = A_BLOCKS_PER_STEP * BLOCK
    assert ctx_len % rows == 0 and (s - ctx_len) % rows == 0
    ctx_blocks = ctx_len // BLOCK
    lat_blocks = (s - ctx_len) // BLOCK
    ctx_pairs = ctx_len // rows
    lat_pairs = (s - ctx_len) // rows
    first_pair = ctx_pairs if final else 0
    npair = s // rows - first_pair

    def lat(i):
        return jnp.maximum(i + first_pair - ctx_pairs, 0)

    def prev_blk(i):
        return ctx_blocks + jnp.maximum(A_BLOCKS_PER_STEP * lat(i) - 1, 0)

    def next_blk(i):
        return ctx_blocks + jnp.minimum(A_BLOCKS_PER_STEP * (lat(i) + 1), lat_blocks - 1)

    own = lambda bb, i: (bb, i + first_pair, 0)
    own_t = lambda bb, i: (bb, 0, i + first_pair)
    body = functools.partial(_attn_a_body, first_pair=first_pair, ctx_pairs=ctx_pairs,
                             lat_pairs=lat_pairs, final=final)
    const2 = lambda bb, i: (0, 0)
    modrow = lambda bb, i: (jnp.where(i + first_pair < ctx_pairs, bsz, bb), 0, 0)
    extra_in, extra_specs = ((final_g,), (pl.BlockSpec((1, d), const2),)) if final else ((), ())
    return pl.pallas_call(
        body,
        out_shape=jax.ShapeDtypeStruct((b, npair * rows, d), F32),
        grid=(b, npair),
        in_specs=[pl.BlockSpec(memory_space=pltpu.SMEM),
                  pl.BlockSpec((1, qw, rows), own_t),
                  pl.BlockSpec((1, BLOCK, kw), lambda bb, i: (bb, prev_blk(i), 0)),
                  pl.BlockSpec((1, rows, kw), own),
                  pl.BlockSpec((1, BLOCK, kw), lambda bb, i: (bb, next_blk(i), 0)),
                  pl.BlockSpec((1, ctx_len, kw), lambda bb, i: (bb, 0, 0)),
                  pl.BlockSpec((1, kw, BLOCK), lambda bb, i: (bb, 0, prev_blk(i))),
                  pl.BlockSpec((1, kw, rows), own_t),
                  pl.BlockSpec((1, kw, BLOCK), lambda bb, i: (bb, 0, next_blk(i))),
                  pl.BlockSpec((1, kw, ctx_len), lambda bb, i: (bb, 0, 0)),
                  pl.BlockSpec((1, rows, qw), own),
                  pl.BlockSpec((1, rows, d), own),
                  pl.BlockSpec(wo_bf.shape, const2),
                  pl.BlockSpec((1, 3, d), modrow),
                  *extra_specs],
        out_specs=pl.BlockSpec((1, rows, d), lambda bb, i: (bb, i, 0)),
        scratch_shapes=[pltpu.VMEM((2, ctx_len + 3 * BLOCK, A_GROUP * BLOCK), F32)],
        input_output_aliases={} if final else {11: 0},
        compiler_params=_cparams(("parallel", "arbitrary")),
        name="attn_a",
    )(sink, q_t, k, k, k, k, v_t, v_t, v_t, v_t, sg, xc, wo_bf, mod, *extra_in)


def _proj_b_body(x_ref, mod_ref, g_ref, w_ref, qn_ref, wuqt_ref, kvn_ref, wkn_ref, wvt_ref,
                 cqt_ref, sqt_ref, ck_ref, sk_ref, qt_ref, k_ref, vt_ref, sg_ref, *, q_scale):
    tm = x_ref.shape[1]
    q_lora = qn_ref.shape[1]
    kv_lora = kvn_ref.shape[1]
    h = _norm_mod(x_ref[0], g_ref[...], mod_ref[0]).astype(BF16)
    y = _dot(h, w_ref[...])
    lane = lax.broadcasted_iota(jnp.int32, (tm, LANES), 1)
    even = (((lane - B_NOPE) // (B_ROPE // 4)) % 2 == 0)
    low = lane < B_NOPE

    def rms(u, gain):
        return u * lax.rsqrt(jnp.mean(u * u, axis=-1, keepdims=True) + EPS) * gain

    cq = rms(y[:, :q_lora], qn_ref[...]).astype(BF16)
    ckv = rms(y[:, q_lora:q_lora + kv_lora], kvn_ref[...]).astype(BF16)
    krg = _rope128(y[:, q_lora + kv_lora:q_lora + kv_lora + LANES], ck_ref[...], sk_ref[...],
                   B_ROPE // 4, even)
    qt = _dot_nt(wuqt_ref[...], cq)
    cqt, sqt = cqt_ref[...], sqt_ref[...]
    for hd in range(B_HEADS):
        r0 = hd * LANES
        qt_ref[0, r0:r0 + B_NOPE, :] = (qt[r0:r0 + B_NOPE] * q_scale).astype(BF16)
        qt_ref[0, r0 + B_NOPE:r0 + B_NOPE + B_ROPE, :] = _rope_rows(
            qt[r0 + B_NOPE:r0 + B_NOPE + B_ROPE], cqt, sqt).astype(BF16)
        qt_ref[0, r0 + B_NOPE + B_ROPE:r0 + LANES, :] = qt[r0 + B_NOPE + B_ROPE:r0 + LANES].astype(BF16)
    vt_ref[0] = _dot_nt(wvt_ref[...], ckv).astype(BF16)
    kn = _dot(ckv, wkn_ref[...])
    for hd in range(B_HEADS):
        grp = kn[:, (hd // 2) * LANES:(hd // 2 + 1) * LANES]
        if hd % 2 == 1:
            grp = pltpu.roll(grp, B_NOPE, 1)
        k_ref[0, :, hd * LANES:(hd + 1) * LANES] = jnp.where(low, grp, krg).astype(BF16)
    sg_ref[0] = _silu(y[:, q_lora + kv_lora + LANES:]).astype(BF16)


def _proj_b(xc, mod, g, w_bf, qn, wuqt_bf, kvn, wkn_bf, wvt_bf, tabs, q_scale, n_ctx_tiles, bsz):
    b, s, d = xc.shape
    tm = ROW_TILE
    hw = B_HEADS * LANES
    gw = B_HEADS * B_V
    row = lambda bb, i: (bb, i, 0)
    modrow = lambda bb, i: (jnp.where(i < n_ctx_tiles, bsz, bb), 0, 0)
    const2 = lambda bb, i: (0, 0)
    tab = pl.BlockSpec((tm, LANES), lambda bb, i: (i, 0))
    tab_t = pl.BlockSpec((B_ROPE, tm), lambda bb, i: (0, i))
    big = jax.ShapeDtypeStruct((b, s, hw), BF16)
    full = lambda a: pl.BlockSpec(a.shape, const2)
    return pl.pallas_call(
        functools.partial(_proj_b_body, q_scale=q_scale),
        out_shape=(jax.ShapeDtypeStruct((b, hw, s), BF16), big, jax.ShapeDtypeStruct((b, gw, s), BF16), jax.ShapeDtypeStruct((b, s, gw), BF16)),
        grid=(b, s // tm),
        in_specs=[pl.BlockSpec((1, tm, d), row),
                  pl.BlockSpec((1, 3, d), modrow),
                  pl.BlockSpec((1, d), const2),
                  full(w_bf), full(qn), full(wuqt_bf), full(kvn), full(wkn_bf), full(wvt_bf),
                  tab_t, tab_t, tab, tab],
        out_specs=(pl.BlockSpec((1, hw, tm), lambda bb, i: (bb, 0, i)), pl.BlockSpec((1, tm, hw), row),
                   pl.BlockSpec((1, gw, tm), lambda bb, i: (bb, 0, i)), pl.BlockSpec((1, tm, gw), row)),
        compiler_params=_cparams(("parallel", "arbitrary")),
        name="proj_b",
    )(xc, mod, g, w_bf, qn, wuqt_bf, kvn, wkn_bf, wvt_bf, *tabs)


B_KEY_CHUNK = 256


def _attn_b_body(qt_ref, k_ref, vt_ref, sg_ref, x_ref, wo_ref, mod_ref, o_ref, s_ref,
                 *, ctx_len, n_ctx_tiles):
    i = pl.program_id(1)
    nh = B_HEADS
    kc = B_KEY_CHUNK

    def attend(nk):
        nc = nk // kc

        def scores(hd, c):
            sl = slice(hd * LANES, (hd + 1) * LANES)
            s_t = _dot(k_ref[0, c * kc:(c + 1) * kc, sl], qt_ref[0, sl, :])
            s_ref[hd % 2, c * kc:(c + 1) * kc, :] = s_t
            return jnp.max(s_t, axis=0, keepdims=True)

        def weighted(hd, c, m):
            p = jnp.exp2(s_ref[hd % 2, c * kc:(c + 1) * kc, :] - m).astype(BF16)
            v1 = _with_ones(vt_ref[0, hd * B_V:(hd + 1) * B_V, c * kc:(c + 1) * kc])
            return _dot(v1, p)

        outs = []
        mx = [scores(0, c) for c in range(nc)]
        for hd in range(nh):
            m = functools.reduce(jnp.maximum, mx)
            mx = []
            ov = None
            for c in range(nc):
                part = weighted(hd, c, m)
                ov = part if ov is None else ov + part
                if hd + 1 < nh:
                    mx.append(scores(hd + 1, c))
            outs.append(ov[:B_V] * (1.0 / ov[B_V:B_V + 1]))
        o_t = jnp.concatenate(outs, axis=0)
        act = (o_t.T * sg_ref[0].astype(F32)).astype(BF16)
        o_ref[0] = x_ref[0] + mod_ref[0][2:3] * _dot(act, wo_ref[...])

    pl.when(i < n_ctx_tiles)(lambda: attend(ctx_len))
    pl.when(i >= n_ctx_tiles)(lambda: attend(k_ref.shape[1]))


def _attn_b(q_t, kpad, v_t, sg, xc, wo_bf, mod, ctx_len, bsz):
    b, s, hw = kpad.shape
    d = xc.shape[2]
    tq = ROW_TILE
    n_ctx_tiles = ctx_len // tq
    body = functools.partial(_attn_b_body, ctx_len=ctx_len, n_ctx_tiles=n_ctx_tiles)
    row = lambda bb, i: (bb, i, 0)
    whole = lambda bb, i: (bb, 0, 0)
    return pl.pallas_call(
        body,
        out_shape=jax.ShapeDtypeStruct(xc.shape, F32),
        grid=(b, s // tq),
        in_specs=[pl.BlockSpec((1, hw, tq), lambda bb, i: (bb, 0, i)),
                  pl.BlockSpec((1, s, hw), whole),
                  pl.BlockSpec((1, v_t.shape[1], s), whole),
                  pl.BlockSpec((1, tq, sg.shape[2]), row),
                  pl.BlockSpec((1, tq, d), row),
                  pl.BlockSpec(wo_bf.shape, lambda bb, i: (0, 0)),
                  pl.BlockSpec((1, 3, d), lambda bb, i: (jnp.where(i < n_ctx_tiles, bsz, bb), 0, 0))],
        out_specs=pl.BlockSpec((1, tq, d), row),
        scratch_shapes=[pltpu.VMEM((2, s, tq), F32)],
        input_output_aliases={4: 0},
        compiler_params=_cparams(("parallel", "arbitrary")),
        name="attn_b",
    )(q_t, kpad, v_t, sg, xc, wo_bf, mod)


C_ROW_TILE = 256


def _proj_c_body(x_ref, modc_ref, modx_ref, g_ref, wdt_ref, dtb_ref, w_ref, cw_ref, cb_ref,
                 z_ref, xbc_ref, dt_ref, ht_ref, *, ctx_len, z_tiles):
    j = pl.program_id(1)
    s = x_ref.shape[1]
    tm = ROW_TILE

    @pl.when(j == 0)
    def _():
        for r in range(s // tm):
            mod = modc_ref[0] if r * tm < ctx_len else modx_ref[0]
            hh = _norm_mod(x_ref[0, r * tm:(r + 1) * tm, :], g_ref[...], mod)
            ht_ref[:, r * tm:(r + 1) * tm] = hh.T.astype(BF16)
        dt = _dot(wdt_ref[...], ht_ref[...]) + dtb_ref[...]
        dt_ref[0] = jnp.maximum(dt, 0.0) + jnp.log1p(jnp.exp(-jnp.abs(dt)))

    @pl.when(j < z_tiles)
    def _():
        z_ref[0] = _silu(_dot(w_ref[...], ht_ref[...])).astype(BF16)

    @pl.when(j >= z_tiles)
    def _():
        t = lax.broadcasted_iota(jnp.int32, (1, s), 1)
        has_prev = (t != 0) & (t != ctx_len)
        has_next = (t != ctx_len - 1) & (t != s - 1)
        half = w_ref.shape[0] // 2
        for r0 in (0, half):
            rs = slice(r0, r0 + half)
            u = _dot(w_ref[rs, :], ht_ref[...])
            up = jnp.where(has_prev, pltpu.roll(u, 1, 1), 0.0)
            un = jnp.where(has_next, pltpu.roll(u, s - 1, 1), 0.0)
            cw = cw_ref[rs, :]
            v = cw[:, 0:1] * up + cw[:, 1:2] * u + cw[:, 2:3] * un + cb_ref[rs, :]
            xbc_ref[0, rs, :] = _silu(v).astype(BF16)


def _proj_c(xc, mod, g, wt_bf, wdt_bf, dtb, cwt, cbt, ctx_len, bsz, inner):
    b, s, d = xc.shape
    tc = C_ROW_TILE
    nrows = wt_bf.shape[0]
    z_tiles = inner // tc
    n_tiles = nrows // tc
    conv_dim = nrows - inner
    ndt = wdt_bf.shape[0]
    body = functools.partial(_proj_c_body, ctx_len=ctx_len, z_tiles=z_tiles)
    const2 = lambda bb, j: (0, 0)
    return pl.pallas_call(
        body,
        out_shape=(jax.ShapeDtypeStruct((b, inner, s), BF16),
                   jax.ShapeDtypeStruct((b, conv_dim, s), BF16),
                   jax.ShapeDtypeStruct((b, ndt, s), F32)),
        grid=(b, n_tiles),
        in_specs=[pl.BlockSpec((1, s, d), lambda bb, j: (bb, 0, 0)),
                  pl.BlockSpec((1, 3, d), lambda bb, j: (bsz, 0, 0)),
                  pl.BlockSpec((1, 3, d), lambda bb, j: (bb, 0, 0)),
                  pl.BlockSpec((1, d), const2),
                  pl.BlockSpec((ndt, d), const2),
                  pl.BlockSpec((ndt, 1), const2),
                  pl.BlockSpec((tc, d), lambda bb, j: (j, 0)),
                  pl.BlockSpec((tc, 3), lambda bb, j: (jnp.maximum(j - z_tiles, 0), 0)),
                  pl.BlockSpec((tc, 1), lambda bb, j: (jnp.maximum(j - z_tiles, 0), 0))],
        out_specs=(pl.BlockSpec((1, tc, s), lambda bb, j: (bb, jnp.minimum(j, z_tiles - 1), 0)),
                   pl.BlockSpec((1, tc, s), lambda bb, j: (bb, jnp.maximum(j - z_tiles, 0), 0)),
                   pl.BlockSpec((1, ndt, s), lambda bb, j: (bb, 0, 0))),
        scratch_shapes=[pltpu.VMEM((d, s), BF16)],
        compiler_params=_cparams(("parallel", "arbitrary")),
        name="proj_c",
    )(xc, mod, mod, g, wdt_bf, dtb, wt_bf, cwt, cbt)


def _ssd_chunk(t, n_chunks, ctx_chunks):
    tb = t - n_chunks
    cb = jnp.where(tb < ctx_chunks, ctx_chunks - 1 - tb, n_chunks - 1 - (tb - ctx_chunks))
    return jnp.where(t < n_chunks, t, cb)


def _rep_rows(a, reps):
    r = a.shape[0]
    return jnp.broadcast_to(a[:, None, :], (r, reps, a.shape[1])).reshape(r * reps, a.shape[1])


def _ssd_body(z_ref, xbc_ref, dt_ref, dtn_ref, a_ref, dsk_ref, nw_ref, x_ref, wo_ref, mod_ref,
              o_ref, yf_ref, st_ref, yb_ref, pro_ref, *, n_chunks, ctx_chunks, heads, inner):
    t = pl.program_id(1)
    q = C_CHUNK
    hpg = heads // C_GROUPS
    gw = hpg * C_HEAD_DIM
    bwd = t >= n_chunks
    d = bwd.astype(jnp.int32)
    c = _ssd_chunk(t, n_chunks, ctx_chunks)
    ii = lax.broadcasted_iota(jnp.int32, (q, q), 0)
    jj = lax.broadcasted_iota(jnp.int32, (q, q), 1)

    def decay_terms(dt_blk_ref, step):
        rev = step >= n_chunks
        rows = pl.ds(pl.multiple_of(rev.astype(jnp.int32) * heads, heads), heads)
        dtd = dt_blk_ref[0, rows, :]
        a = dtd * a_ref[rows, :]
        tri = (ii <= jj).astype(BF16)
        a_hi, a_lo = _split2(a)
        a_lo2 = (a - a_hi.astype(F32) - a_lo.astype(F32)).astype(BF16)
        cum = _dot(a_hi, tri) + _dot(a_lo, tri) + _dot(a_lo2, tri)
        tot = jnp.broadcast_to(cum[:, q - 1:q], cum.shape)
        u = jnp.where(rev, tot - cum + a, cum)
        pro_ref[0] = dtd
        pro_ref[1] = u * LOG2E
        pro_ref[2] = jnp.exp(u)
        pro_ref[3] = dtd * jnp.exp(tot - u)
        pro_ref[4] = jnp.exp(tot)

    @pl.when(t == 0)
    def _():
        decay_terms(dt_ref, t)

    @pl.when((t == 0) | (t == n_chunks))
    def _():
        st_ref[...] = jnp.zeros_like(st_ref)

    dtd, u2, eu, dtdend, etot = (pro_ref[i] for i in range(5))
    decay_terms(dtn_ref, t + 1)
    sgn = 1 - 2 * d
    keep = (jj - ii) * sgn >= 0

    for g in range(C_GROUPS):
        r0 = g * gw
        hs = slice(g * hpg, (g + 1) * hpg)
        b_t = xbc_ref[0, inner + g * C_STATE: inner + (g + 1) * C_STATE, :]
        c_t = xbc_ref[0, inner + (C_GROUPS + g) * C_STATE: inner + (C_GROUPS + g + 1) * C_STATE, :]
        b_g = b_t.astype(F32).T.astype(BF16)
        cb_t = _dot(b_g, c_t)
        xs = xbc_ref[0, r0:r0 + gw, :].astype(F32)
        x_bf = (xs * _rep_rows(dtd[hs], C_HEAD_DIM)).astype(BF16)
        st = st_ref[r0:r0 + gw, :]
        y_off = _dot(st.astype(BF16), c_t) * _rep_rows(eu[hs], C_HEAD_DIM)
        st_ref[r0:r0 + gw, :] = (_rep_rows(etot[hs], C_HEAD_DIM) * st
                                 + _dot((xs * _rep_rows(dtdend[hs], C_HEAD_DIM)).astype(BF16), b_g))
        for hh in range(hpg):
            hd = g * hpg + hh
            r = jnp.broadcast_to(u2[hd:hd + 1, :], (q, q))
            e = jnp.where(keep, r - r.T, NEG)
            m_t = (jnp.exp2(e) * cb_t).astype(BF16)
            rr = slice(hh * C_HEAD_DIM, (hh + 1) * C_HEAD_DIM)
            yb_ref[r0 + hh * C_HEAD_DIM:r0 + (hh + 1) * C_HEAD_DIM, :] = _dot(x_bf[rr], m_t) + y_off[rr]

    @pl.when(jnp.logical_not(bwd))
    def _():
        yf_ref[c] = yb_ref[...] + dsk_ref[...] * xbc_ref[0, 0:inner, :].astype(F32)

    @pl.when(bwd)
    def _():
        acts = []
        for g in range(C_GROUPS):
            r0 = g * gw
            rs = slice(r0, r0 + gw)
            y = yf_ref[c, rs, :] + yb_ref[rs, :]
            y = y * z_ref[0, rs, :].astype(F32)
            ms = jnp.mean(y * y, axis=0, keepdims=True)
            y = y * lax.rsqrt(ms + EPS) * nw_ref[rs, :]
            acts.append(y.T.astype(BF16))
        o_ref[0] = x_ref[0] + mod_ref[0][2:3] * _dot(jnp.concatenate(acts, axis=1), wo_ref[...])


def _ssd(z_t, xbc_t, dt_t, a_tab, dsk_tab, nw_tab, xc, wo_bf, mod, ctx_len, bsz):
    b, inner, s = z_t.shape
    d = xc.shape[2]
    conv_dim = xbc_t.shape[1]
    heads = inner // C_HEAD_DIM
    n_chunks = s // C_CHUNK
    ctx_chunks = ctx_len // C_CHUNK
    body = functools.partial(_ssd_body, n_chunks=n_chunks, ctx_chunks=ctx_chunks, heads=heads,
                             inner=inner)
    cmap = lambda bb, t: (bb, 0, _ssd_chunk(t, n_chunks, ctx_chunks))
    nmap = lambda bb, t: (bb, 0, _ssd_chunk(jnp.minimum(t + 1, 2 * n_chunks - 1), n_chunks, ctx_chunks))
    out_chunk = lambda t: _ssd_chunk(jnp.maximum(t, n_chunks), n_chunks, ctx_chunks)
    omap = lambda bb, t: (bb, out_chunk(t), 0)
    modrow = lambda bb, t: (jnp.where(out_chunk(t) < ctx_chunks, bsz, bb), 0, 0)
    const2 = lambda bb, t: (0, 0)
    return pl.pallas_call(
        body,
        out_shape=jax.ShapeDtypeStruct(xc.shape, F32),
        grid=(b, 2 * n_chunks),
        in_specs=[pl.BlockSpec((1, inner, C_CHUNK), cmap),
                  pl.BlockSpec((1, conv_dim, C_CHUNK), cmap),
                  pl.BlockSpec((1, 2 * heads, C_CHUNK), cmap),
                  pl.BlockSpec((1, 2 * heads, C_CHUNK), nmap),
                  pl.BlockSpec((2 * heads, LANES), const2),
                  pl.BlockSpec((inner, LANES), const2),
                  pl.BlockSpec((inner, LANES), const2),
                  pl.BlockSpec((1, C_CHUNK, d), omap),
                  pl.BlockSpec(wo_bf.shape, const2),
                  pl.BlockSpec((1, 3, d), modrow)],
        out_specs=pl.BlockSpec((1, C_CHUNK, d), omap),
        scratch_shapes=[pltpu.VMEM((n_chunks, inner, C_CHUNK), F32),
                        pltpu.VMEM((inner, C_STATE), F32),
                        pltpu.VMEM((inner, C_CHUNK), F32),
                        pltpu.VMEM((5, heads, C_CHUNK), F32)],
        input_output_aliases={7: 0},
        compiler_params=_cparams(("parallel", "arbitrary")),
        name="ssd_scan",
    )(z_t, xbc_t, dt_t, dt_t, a_tab, dsk_tab, nw_tab, xc, wo_bf, mod)


def _lanes(v):
    return jnp.broadcast_to(v.astype(F32)[:, None], (v.shape[0], LANES))


def kernel(x, c, ctx, c_ctx, ada_w, ada_b, norm_g, final_g, a_w_in, a_sink, a_w_out, b_w_in, b_q_norm, b_w_uq, b_kv_norm, b_w_ukv, b_w_out, c_w_in, c_conv_w, c_conv_b, c_dt_bias, c_a_log, c_d, c_norm, c_w_out):
    bsz, t_len, d = x.shape
    ctx_len = ctx.shape[1]
    depth = ada_w.shape[0]
    assert ctx_len % ROW_TILE == 0 and t_len % ROW_TILE == 0 and t_len % GRID_W == 0
    n_ctx_tiles = ctx_len // ROW_TILE
    rows = t_len // GRID_W

    xc = jnp.concatenate([ctx, x], axis=1)

    r_pad = -(-(bsz + 1) // 8) * 8
    cond = jnp.concatenate([c, c_ctx[None], jnp.zeros((r_pad - bsz - 1, d), F32)], axis=0)
    mods = _adaln_all(cond, ada_w, ada_b).reshape(depth, r_pad, 3, d)

    lane = jnp.arange(LANES)
    a_scale = A_HEAD_DIM ** -0.5 * LOG2E
    a_lane_dim = lane % A_HEAD_DIM
    a_tabs = (_rope_tables_t(rows, A_HEAD_DIM, ctx_len, a_scale)
              + _rope_tables(rows, A_HEAD_DIM, ctx_len, a_lane_dim, 1.0))
    b_scale = (B_NOPE + B_ROPE) ** -0.5 * LOG2E
    b_lane_dim = jnp.where((lane >= B_NOPE) & (lane < B_NOPE + B_ROPE), lane - B_NOPE, -1)
    b_tabs = (_rope_tables_t(rows, B_ROPE, ctx_len, b_scale)
              + _rope_tables(rows, B_ROPE, ctx_len, b_lane_dim, 1.0))

    for i in range(depth):
        kind = i % N_MIXERS
        j = i // N_MIXERS
        last = i == depth - 1
        mod = mods[i]
        g = norm_g[i][None]
        if kind == 0:
            w = a_w_in[j]
            qw, kw = A_HEADS * A_HEAD_DIM, A_KV_HEADS * A_HEAD_DIM
            w_tok = jnp.concatenate([w[:, qw:qw + kw], w[:, qw + 2 * kw:]], axis=1)
            q_t, k, v_t, sg = _proj_a(xc, mod, g, w[:, :qw].T.astype(BF16),
                                      w[:, qw + kw:qw + 2 * kw].T.astype(BF16), w_tok.astype(BF16),
                                      a_tabs, n_ctx_tiles, bsz)
            xc = _attn_a(q_t, k, v_t, sg, a_sink[j].astype(F32), xc, a_w_out[j].astype(BF16), mod,
                         ctx_len, bsz, final_g=final_g[None] if last else None)
        elif kind == 1:
            w = b_w_in[j]
            q_lora = b_q_norm.shape[1]
            kv_lora = b_kv_norm.shape[1]
            k0 = q_lora + kv_lora
            w_pad = jnp.concatenate(
                [w[:, :k0], jnp.zeros((d, B_NOPE), F32), w[:, k0:k0 + B_ROPE],
                 jnp.zeros((d, LANES - B_NOPE - B_ROPE), F32), w[:, k0 + B_ROPE:]], axis=1)
            wuq = b_w_uq[j].reshape(q_lora, B_HEADS, B_NOPE + B_ROPE)
            wuq = jnp.pad(wuq, ((0, 0), (0, 0), (0, LANES - B_NOPE - B_ROPE))).reshape(q_lora, B_HEADS * LANES)
            wukv = b_w_ukv[j].reshape(kv_lora, B_HEADS, B_NOPE + B_V)
            wkn = wukv[:, :, :B_NOPE].reshape(kv_lora, B_HEADS * B_NOPE)
            wv = wukv[:, :, B_NOPE:].reshape(kv_lora, B_HEADS * B_V)
            q_t, kp, v_t, sg = _proj_b(xc, mod, g, w_pad.astype(BF16), b_q_norm[j][None],
                                       wuq.T.astype(BF16), b_kv_norm[j][None], wkn.astype(BF16),
                                       wv.T.astype(BF16), b_tabs, b_scale, n_ctx_tiles, bsz)
            xc = _attn_b(q_t, kp, v_t, sg, xc, b_w_out[j].astype(BF16), mod, ctx_len, bsz)
        else:
            inner = c_norm.shape[1]
            heads = inner // C_HEAD_DIM
            conv_dim = c_conv_w.shape[2]
            w = c_w_in[j]
            wt = w[:, :inner + conv_dim].T.astype(BF16)
            wdt = w[:, inner + conv_dim:].T.astype(BF16)
            z_t, xbc_t, dt_t = _proj_c(xc, mod, g, wt, wdt, c_dt_bias[j].reshape(2 * heads, 1),
                                       c_conv_w[j].T, c_conv_b[j][:, None], ctx_len, bsz, inner)
            a_tab = _lanes(-jnp.exp(c_a_log[j].astype(F32)).reshape(2 * heads))
            dsk_tab = _lanes(jnp.repeat(c_d[j], C_HEAD_DIM))
            xc = _ssd(z_t, xbc_t, dt_t, a_tab, dsk_tab, _lanes(c_norm[j]), xc, c_w_out[j].astype(BF16), mod,
                      ctx_len, bsz)
    assert (depth - 1) % N_MIXERS == 0
    return xc
```

```python
import functools
import math

import jax
import jax.numpy as jnp
from jax import lax
from jax.experimental import pallas as pl
from jax.experimental.pallas import tpu as pltpu

F32 = jnp.float32
BF16 = jnp.bfloat16

GRID_W = 64
N_MIXERS = 3
ROPE_BASE = 10000.0
EPS = 1e-6
BLOCK = 128
WINDOW = 128

A_HEADS = 16
A_KV_HEADS = 4
A_GROUP = A_HEADS // A_KV_HEADS
A_HEAD_DIM = 64

B_HEADS = 16
B_NOPE = 64
B_ROPE = 32
B_V = 64

C_HEAD_DIM = 64
C_GROUPS = 4
C_STATE = 128
C_CHUNK = 128

LANES = 128
BF16_ROWS = 16
ROW_TILE = 256
VMEM_LIMIT = 56 * 1024 * 1024
NEG = -1e30
LOG2E = math.log2(math.e)


def _cparams(sem):
    return pltpu.CompilerParams(dimension_semantics=sem, vmem_limit_bytes=VMEM_LIMIT)


def _dot(a, b):
    return jnp.dot(a, b, preferred_element_type=F32)


def _dot_nt(a, b):
    return lax.dot_general(a, b, (((1,), (1,)), ((), ())), preferred_element_type=F32)


def _silu(v):
    return v / (1.0 + jnp.exp(-v))


def _split2(a):
    hi = a.astype(BF16)
    lo = (a - hi.astype(F32)).astype(BF16)
    return hi, lo


def _dot_f32(a, b):
    a_hi, a_lo = _split2(a)
    b_hi, b_lo = _split2(b)
    return _dot(a_hi, b_hi) + _dot(a_hi, b_lo) + _dot(a_lo, b_hi)


def _norm_mod(x, g, mod):
    ms = jnp.mean(x * x, axis=-1, keepdims=True)
    y = x * lax.rsqrt(ms + EPS) * g
    return y * (1.0 + mod[1:2]) + mod[0:1]


def _with_ones(v_t):
    return jnp.concatenate([v_t, jnp.ones((BF16_ROWS, v_t.shape[1]), BF16)], axis=0)


def _adaln_body(cond_ref, w_ref, b_ref, o_ref):
    o_ref[0] = _dot_f32(_silu(cond_ref[...]), w_ref[0]) + b_ref[0]


def _adaln_all(cond, ada_w, ada_b):
    depth, d, _ = ada_w.shape
    r = cond.shape[0]
    return pl.pallas_call(
        _adaln_body,
        out_shape=jax.ShapeDtypeStruct((depth, r, 3 * d), F32),
        grid=(depth, 3),
        in_specs=[pl.BlockSpec((r, d), lambda i, j: (0, 0)),
                  pl.BlockSpec((1, d, d), lambda i, j: (i, 0, j)),
                  pl.BlockSpec((1, 1, d), lambda i, j: (i, 0, j))],
        out_specs=pl.BlockSpec((1, r, d), lambda i, j: (i, 0, j)),
        compiler_params=_cparams(("arbitrary", "arbitrary")),
        name="adaln_mod",
    )(cond, ada_w, ada_b.reshape(depth, 1, 3 * d))


def _rope_tables(rows, dim, ctx_len, lane_of_dim, scale):
    row = jnp.repeat(jnp.arange(rows), GRID_W).astype(F32)
    col = (jnp.arange(rows * GRID_W) % GRID_W).astype(F32)
    nf = dim // 4
    inv = ROPE_BASE ** (-jnp.arange(nf, dtype=F32) / nf)
    ar = row[:, None] * inv
    ac = col[:, None] * inv
    ang = jnp.concatenate([ar, ar, ac, ac], axis=-1)
    cos, sin = jnp.cos(ang), jnp.sin(ang)
    sign = jnp.where((jnp.arange(dim) // nf) % 2 == 0, -1.0, 1.0).astype(F32)
    sin = sin * sign
    idx = jnp.asarray(lane_of_dim)
    valid = (idx >= 0)[None, :]
    cos_l = jnp.where(valid, cos[:, jnp.maximum(idx, 0)], 1.0)
    sin_l = jnp.where(valid, sin[:, jnp.maximum(idx, 0)], 0.0)
    cos_l = jnp.concatenate([jnp.ones((ctx_len, LANES), F32), cos_l], axis=0)
    sin_l = jnp.concatenate([jnp.zeros((ctx_len, LANES), F32), sin_l], axis=0)
    return cos_l * scale, sin_l * scale


def _rope128(u, cos, sin, quarter, even):
    fwd = pltpu.roll(u, quarter, 1)
    bwd = pltpu.roll(u, LANES - quarter, 1)
    return u * cos + jnp.where(even, bwd, fwd) * sin


def _rope_tables_t(rows, dim, ctx_len, scale):
    row = jnp.repeat(jnp.arange(rows), GRID_W).astype(F32)
    col = (jnp.arange(rows * GRID_W) % GRID_W).astype(F32)
    nf = dim // 4
    inv = ROPE_BASE ** (-jnp.arange(nf, dtype=F32) / nf)
    ar = inv[:, None] * row[None, :]
    ac = inv[:, None] * col[None, :]
    ang = jnp.concatenate([ar, ar, ac, ac], axis=0)
    cos = jnp.concatenate([jnp.ones((dim, ctx_len), F32), jnp.cos(ang)], axis=1)
    sin = jnp.concatenate([jnp.zeros((dim, ctx_len), F32), jnp.sin(ang)], axis=1)
    return cos * scale, sin * scale


def _rope_rows(u, cos_t, sin_t):
    qd = u.shape[0] // 4
    r1, r2, c1, c2 = (u[i * qd:(i + 1) * qd] for i in range(4))
    rot = jnp.concatenate([-r2, r1, -c2, c1], axis=0)
    return u * cos_t + rot * sin_t


def _step_rows(x_refs, is_ctx):
    if len(x_refs) == 1:
        return x_refs[0][0]
    return jnp.where(is_ctx, x_refs[0][0], x_refs[1][0])


def _row_specs(xs, rows, n_ctx_steps, first=0):
    d = xs[0].shape[2]
    if len(xs) == 1:
        return [pl.BlockSpec((1, rows, d), lambda bb, i: (bb, i + first, 0))]
    return [pl.BlockSpec((1, rows, d), lambda bb, i: (bb, jnp.minimum(i + first, n_ctx_steps - 1), 0)),
            pl.BlockSpec((1, rows, d), lambda bb, i: (bb, jnp.maximum(i + first - n_ctx_steps, 0), 0))]


def _proj_a_body(*refs, n_x, n_ctx_tiles):
    x_refs = refs[:n_x]
    (mod_ref, g_ref, wqt_ref, wvt_ref, w_ref, cqt_ref, sqt_ref, ck_ref, sk_ref,
     qt_ref, k_ref, vt_ref, sg_ref) = refs[n_x:]
    tm = x_refs[0].shape[1]
    x_rows = _step_rows(x_refs, pl.program_id(1) < n_ctx_tiles)
    h = _norm_mod(x_rows, g_ref[...], mod_ref[0]).astype(BF16)
    kw = A_KV_HEADS * A_HEAD_DIM
    qt = _dot_nt(wqt_ref[...], h)
    cqt, sqt = cqt_ref[...], sqt_ref[...]
    for hd in range(A_HEADS):
        rs = slice(hd * A_HEAD_DIM, (hd + 1) * A_HEAD_DIM)
        qt_ref[0, rs, :] = _rope_rows(qt[rs], cqt, sqt).astype(BF16)
    vt_ref[0] = _dot_nt(wvt_ref[...], h).astype(BF16)
    y = _dot(h, w_ref[...])
    lane = lax.broadcasted_iota(jnp.int32, (tm, LANES), 1)
    even = ((lane % A_HEAD_DIM) // (A_HEAD_DIM // 4)) % 2 == 0
    ck, sk = ck_ref[...], sk_ref[...]
    for j in range(kw // LANES):
        kg = _rope128(y[:, j * LANES:(j + 1) * LANES], ck, sk, A_HEAD_DIM // 4, even)
        k_ref[0, :, j * LANES:(j + 1) * LANES] = kg.astype(BF16)
    sg_ref[0] = _silu(y[:, kw:]).astype(BF16)


def _proj_a(xs, mod, g, wqt_bf, wvt_bf, w_bf, tabs, n_ctx_tiles, bsz):
    b, _, d = xs[0].shape
    s = sum(a.shape[1] for a in xs)
    tm = ROW_TILE
    qw = A_HEADS * A_HEAD_DIM
    kw = A_KV_HEADS * A_HEAD_DIM
    row = lambda bb, i: (bb, i, 0)
    modrow = lambda bb, i: (jnp.where(i < n_ctx_tiles, bsz, bb), 0, 0)
    const2 = lambda bb, i: (0, 0)
    full = lambda a: pl.BlockSpec(a.shape, const2)
    tab = pl.BlockSpec((tm, LANES), lambda bb, i: (i, 0))
    tab_t = pl.BlockSpec((A_HEAD_DIM, tm), lambda bb, i: (0, i))
    wide = jax.ShapeDtypeStruct((b, s, qw), BF16)
    return pl.pallas_call(
        functools.partial(_proj_a_body, n_x=len(xs), n_ctx_tiles=n_ctx_tiles),
        out_shape=(jax.ShapeDtypeStruct((b, qw, s), BF16), jax.ShapeDtypeStruct((b, s, kw), BF16),
                   jax.ShapeDtypeStruct((b, kw, s), BF16), wide),
        grid=(b, s // tm),
        in_specs=[*_row_specs(xs, tm, n_ctx_tiles),
                  pl.BlockSpec((1, 3, d), modrow),
                  pl.BlockSpec((1, d), const2),
                  full(wqt_bf), full(wvt_bf), full(w_bf),
                  tab_t, tab_t, tab, tab],
        out_specs=(pl.BlockSpec((1, qw, tm), lambda bb, i: (bb, 0, i)), pl.BlockSpec((1, tm, kw), row),
                   pl.BlockSpec((1, kw, tm), lambda bb, i: (bb, 0, i)), pl.BlockSpec((1, tm, qw), row)),
        compiler_params=_cparams(("parallel", "arbitrary")),
        name="proj_a",
    )(*xs, mod, g, wqt_bf, wvt_bf, w_bf, *tabs)


A_BLOCKS_PER_STEP = 2


def _attn_a_body(sink_ref, qt_ref, kp_ref, ko_ref, kn_ref, kc_ref, vp_ref, vo_ref, vn_ref, vc_ref,
                 sg_ref, *rest, n_x, first_pair, ctx_pairs, lat_pairs, final):
    x_refs, rest = rest[:n_x], rest[n_x:]
    if final:
        wo_ref, mod_ref, fg_ref, o_ref, s_ref = rest
    else:
        wo_ref, mod_ref, o_ref, s_ref = rest
    jp = pl.program_id(1) + first_pair - ctx_pairs
    nq = BLOCK
    nsub = A_BLOCKS_PER_STEP
    cols = A_GROUP * nq
    colq = lax.broadcasted_iota(jnp.int32, (1, cols), 1)
    nctx = kc_ref.shape[1]

    def run(local):
        if local:
            kk = lax.broadcasted_iota(jnp.int32, (nq, cols), 0)
            qi = lax.broadcasted_iota(jnp.int32, (nq, cols), 1) % nq
            far = 4 * nq
            no_prev = jnp.where(jp > 0, 0, far)
            no_next = jnp.where(jp < lat_pairs - 1, 0, far)
            every = jnp.full((nq, cols), True)
            ok = [jnp.concatenate([kk - qi >= no_prev, every, kk <= qi], axis=0),
                  jnp.concatenate([kk >= qi, every, qi - kk >= no_next], axis=0)]

        def k_local(sub, hl):
            own = [ko_ref[0, 0:nq, hl], ko_ref[0, nq:2 * nq, hl]]
            return [kp_ref[0, :, hl]] + own if sub == 0 else own + [kn_ref[0, :, hl]]

        def v_local(sub, hr):
            own = [vo_ref[0, hr, 0:nq], vo_ref[0, hr, nq:2 * nq]]
            return [vp_ref[0, hr, :]] + own if sub == 0 else own + [vn_ref[0, hr, :]]

        units = [(sub, h) for sub in range(nsub) for h in range(A_KV_HEADS)]

        def sink_row(h):
            row = jnp.zeros((1, cols), F32)
            for g in range(A_GROUP):
                row = jnp.where(colq // nq == g, sink_ref[h * A_GROUP + g], row)
            return row * LOG2E

        def scores(u, part):
            sub, h = units[u]
            q4 = jnp.concatenate(
                [qt_ref[0, (h * A_GROUP + g) * A_HEAD_DIM:(h * A_GROUP + g + 1) * A_HEAD_DIM, sub * nq:(sub + 1) * nq]
                 for g in range(A_GROUP)], axis=1)
            zero = jnp.zeros_like(q4)
            rhs = jnp.concatenate([q4, zero] if h % 2 == 0 else [zero, q4], axis=0)
            hl = slice((h // 2) * LANES, (h // 2 + 1) * LANES)
            if part == 0:
                s_t = _dot(kc_ref[0, :, hl], rhs)
                s_ref[u % 2, 0:nctx, :] = s_t
            else:
                s_t = jnp.where(ok[sub], _dot(jnp.concatenate(k_local(sub, hl), axis=0), rhs), NEG)
                s_ref[u % 2, nctx:nctx + 3 * nq, :] = s_t
            return jnp.max(s_t, axis=0, keepdims=True)

        def weighted(u, part, m):
            sub, h = units[u]
            hr = slice(h * A_HEAD_DIM, (h + 1) * A_HEAD_DIM)
            if part == 0:
                v1 = _with_ones(vc_ref[0, hr, :])
                p = jnp.exp2(s_ref[u % 2, 0:nctx, :] - m)
            else:
                v1 = _with_ones(jnp.concatenate(v_local(sub, hr), axis=1))
                p = jnp.exp2(s_ref[u % 2, nctx:nctx + 3 * nq, :] - m)
            return _dot(v1, p.astype(BF16))

        parts = (0, 1) if local else (0,)
        acts = [[] for _ in range(nsub)]
        mx = [scores(0, part) for part in parts]
        for u, (sub, h) in enumerate(units):
            snk = sink_row(h)
            m = functools.reduce(jnp.maximum, mx + [snk])
            mx = []
            ov = None
            for part in parts:
                piece = weighted(u, part, m)
                ov = piece if ov is None else ov + piece
                if u + 1 < len(units):
                    mx.append(scores(u + 1, part))
            den = ov[A_HEAD_DIM:A_HEAD_DIM + 1] + jnp.exp2(snk - m)
            out = ov[:A_HEAD_DIM] * (1.0 / den)
            c0 = h * A_GROUP * A_HEAD_DIM
            for pair in range(A_GROUP // 2):
                pc = slice(c0 + pair * LANES, c0 + (pair + 1) * LANES)
                two = jnp.concatenate([out[:, (2 * pair) * nq:(2 * pair + 1) * nq],
                                       out[:, (2 * pair + 1) * nq:(2 * pair + 2) * nq]], axis=0).T
                gate = sg_ref[0, sub * nq:(sub + 1) * nq, pc].astype(F32)
                acts[sub].append((two * gate).astype(BF16))
        act = jnp.concatenate([jnp.concatenate(a, axis=1) for a in acts], axis=0)
        xn = _step_rows(x_refs, jp < 0) + mod_ref[0][2:3] * _dot(act, wo_ref[...])
        if final:
            xn = xn * lax.rsqrt(jnp.mean(xn * xn, axis=-1, keepdims=True) + EPS) * fg_ref[...]
        o_ref[0] = xn

    if first_pair >= ctx_pairs:
        run(True)
    else:
        pl.when(jp >= 0)(lambda: run(True))
        pl.when(jp < 0)(lambda: run(False))


def _attn_a(q_t, k, v_t, sg, sink, xs, wo_bf, mod, ctx_len, bsz, final_g=None):
    b, qw, s = q_t.shape
    d = xs[0].shape[2]
    kw = v_t.shape[1]
    final = final_g is not None
    rows = A_BLOCKS_PER_STEP * BLOCK
    assert ctx_len % rows == 0 and (s - ctx_len) % rows == 0
    ctx_blocks = ctx_len // BLOCK
    lat_blocks = (s - ctx_len) // BLOCK
    ctx_pairs = ctx_len // rows
    lat_pairs = (s - ctx_len) // rows
    first_pair = ctx_pairs if final else 0
    npair = s // rows - first_pair

    def lat(i):
        return jnp.maximum(i + first_pair - ctx_pairs, 0)

    def prev_blk(i):
        return ctx_blocks + jnp.maximum(A_BLOCKS_PER_STEP * lat(i) - 1, 0)

    def next_blk(i):
        return ctx_blocks + jnp.minimum(A_BLOCKS_PER_STEP * (lat(i) + 1), lat_blocks - 1)

    own = lambda bb, i: (bb, i + first_pair, 0)
    own_t = lambda bb, i: (bb, 0, i + first_pair)
    body = functools.partial(_attn_a_body, n_x=len(xs), first_pair=first_pair, ctx_pairs=ctx_pairs,
                             lat_pairs=lat_pairs, final=final)
    const2 = lambda bb, i: (0, 0)
    modrow = lambda bb, i: (jnp.where(i + first_pair < ctx_pairs, bsz, bb), 0, 0)
    extra_in, extra_specs = ((final_g,), (pl.BlockSpec((1, d), const2),)) if final else ((), ())
    return pl.pallas_call(
        body,
        out_shape=jax.ShapeDtypeStruct((b, npair * rows, d), F32),
        grid=(b, npair),
        in_specs=[pl.BlockSpec(memory_space=pltpu.SMEM),
                  pl.BlockSpec((1, qw, rows), own_t),
                  pl.BlockSpec((1, BLOCK, kw), lambda bb, i: (bb, prev_blk(i), 0)),
                  pl.BlockSpec((1, rows, kw), own),
                  pl.BlockSpec((1, BLOCK, kw), lambda bb, i: (bb, next_blk(i), 0)),
                  pl.BlockSpec((1, ctx_len, kw), lambda bb, i: (bb, 0, 0)),
                  pl.BlockSpec((1, kw, BLOCK), lambda bb, i: (bb, 0, prev_blk(i))),
                  pl.BlockSpec((1, kw, rows), own_t),
                  pl.BlockSpec((1, kw, BLOCK), lambda bb, i: (bb, 0, next_blk(i))),
                  pl.BlockSpec((1, kw, ctx_len), lambda bb, i: (bb, 0, 0)),
                  pl.BlockSpec((1, rows, qw), own),
                  *_row_specs(xs, rows, ctx_pairs, first_pair),
                  pl.BlockSpec(wo_bf.shape, const2),
                  pl.BlockSpec((1, 3, d), modrow),
                  *extra_specs],
        out_specs=pl.BlockSpec((1, rows, d), lambda bb, i: (bb, i, 0)),
        scratch_shapes=[pltpu.VMEM((2, ctx_len + 3 * BLOCK, A_GROUP * BLOCK), F32)],
        input_output_aliases={11: 0} if (len(xs) == 1 and not final) else {},
        compiler_params=_cparams(("parallel", "arbitrary")),
        name="attn_a",
    )(sink, q_t, k, k, k, k, v_t, v_t, v_t, v_t, sg, *xs, wo_bf, mod, *extra_in)


def _proj_b_body(x_ref, mod_ref, g_ref, w_ref, qn_ref, wuqt_ref, kvn_ref, wkn_ref, wvt_ref,
                 cqt_ref, sqt_ref, ck_ref, sk_ref, qt_ref, k_ref, vt_ref, sg_ref, *, q_scale):
    tm = x_ref.shape[1]
    q_lora = qn_ref.shape[1]
    kv_lora = kvn_ref.shape[1]
    h = _norm_mod(x_ref[0], g_ref[...], mod_ref[0]).astype(BF16)
    y = _dot(h, w_ref[...])
    lane = lax.broadcasted_iota(jnp.int32, (tm, LANES), 1)
    even = (((lane - B_NOPE) // (B_ROPE // 4)) % 2 == 0)
    low = lane < B_NOPE

    def rms(u, gain):
        return u * lax.rsqrt(jnp.mean(u * u, axis=-1, keepdims=True) + EPS) * gain

    cq = rms(y[:, :q_lora], qn_ref[...]).astype(BF16)
    ckv = rms(y[:, q_lora:q_lora + kv_lora], kvn_ref[...]).astype(BF16)
    krg = _rope128(y[:, q_lora + kv_lora:q_lora + kv_lora + LANES], ck_ref[...], sk_ref[...],
                   B_ROPE // 4, even)
    qt = _dot_nt(wuqt_ref[...], cq)
    cqt, sqt = cqt_ref[...], sqt_ref[...]
    for hd in range(B_HEADS):
        r0 = hd * LANES
        qt_ref[0, r0:r0 + B_NOPE, :] = (qt[r0:r0 + B_NOPE] * q_scale).astype(BF16)
        qt_ref[0, r0 + B_NOPE:r0 + B_NOPE + B_ROPE, :] = _rope_rows(
            qt[r0 + B_NOPE:r0 + B_NOPE + B_ROPE], cqt, sqt).astype(BF16)
        qt_ref[0, r0 + B_NOPE + B_ROPE:r0 + LANES, :] = qt[r0 + B_NOPE + B_ROPE:r0 + LANES].astype(BF16)
    vt_ref[0] = _dot_nt(wvt_ref[...], ckv).astype(BF16)
    kn = _dot(ckv, wkn_ref[...])
    for hd in range(B_HEADS):
        grp = kn[:, (hd // 2) * LANES:(hd // 2 + 1) * LANES]
        if hd % 2 == 1:
            grp = pltpu.roll(grp, B_NOPE, 1)
        k_ref[0, :, hd * LANES:(hd + 1) * LANES] = jnp.where(low, grp, krg).astype(BF16)
    sg_ref[0] = _silu(y[:, q_lora + kv_lora + LANES:]).astype(BF16)


def _proj_b(xc, mod, g, w_bf, qn, wuqt_bf, kvn, wkn_bf, wvt_bf, tabs, q_scale, n_ctx_tiles, bsz):
    b, s, d = xc.shape
    tm = ROW_TILE
    hw = B_HEADS * LANES
    gw = B_HEADS * B_V
    row = lambda bb, i: (bb, i, 0)
    modrow = lambda bb, i: (jnp.where(i < n_ctx_tiles, bsz, bb), 0, 0)
    const2 = lambda bb, i: (0, 0)
    tab = pl.BlockSpec((tm, LANES), lambda bb, i: (i, 0))
    tab_t = pl.BlockSpec((B_ROPE, tm), lambda bb, i: (0, i))
    big = jax.ShapeDtypeStruct((b, s, hw), BF16)
    full = lambda a: pl.BlockSpec(a.shape, const2)
    return pl.pallas_call(
        functools.partial(_proj_b_body, q_scale=q_scale),
        out_shape=(jax.ShapeDtypeStruct((b, hw, s), BF16), big, jax.ShapeDtypeStruct((b, gw, s), BF16), jax.ShapeDtypeStruct((b, s, gw), BF16)),
        grid=(b, s // tm),
        in_specs=[pl.BlockSpec((1, tm, d), row),
                  pl.BlockSpec((1, 3, d), modrow),
                  pl.BlockSpec((1, d), const2),
                  full(w_bf), full(qn), full(wuqt_bf), full(kvn), full(wkn_bf), full(wvt_bf),
                  tab_t, tab_t, tab, tab],
        out_specs=(pl.BlockSpec((1, hw, tm), lambda bb, i: (bb, 0, i)), pl.BlockSpec((1, tm, hw), row),
                   pl.BlockSpec((1, gw, tm), lambda bb, i: (bb, 0, i)), pl.BlockSpec((1, tm, gw), row)),
        compiler_params=_cparams(("parallel", "arbitrary")),
        name="proj_b",
    )(xc, mod, g, w_bf, qn, wuqt_bf, kvn, wkn_bf, wvt_bf, *tabs)


B_KEY_CHUNK = 256


def _attn_b_body(qt_ref, k_ref, vt_ref, sg_ref, x_ref, wo_ref, mod_ref, o_ref, s_ref,
                 *, ctx_len, n_ctx_tiles):
    i = pl.program_id(1)
    nh = B_HEADS
    kc = B_KEY_CHUNK

    def attend(nk):
        nc = nk // kc

        def scores(hd, c):
            sl = slice(hd * LANES, (hd + 1) * LANES)
            s_t = _dot(k_ref[0, c * kc:(c + 1) * kc, sl], qt_ref[0, sl, :])
            s_ref[hd % 2, c * kc:(c + 1) * kc, :] = s_t
            return jnp.max(s_t, axis=0, keepdims=True)

        def weighted(hd, c, m):
            p = jnp.exp2(s_ref[hd % 2, c * kc:(c + 1) * kc, :] - m).astype(BF16)
            v1 = _with_ones(vt_ref[0, hd * B_V:(hd + 1) * B_V, c * kc:(c + 1) * kc])
            return _dot(v1, p)

        outs = []
        mx = [scores(0, c) for c in range(nc)]
        for hd in range(nh):
            m = functools.reduce(jnp.maximum, mx)
            mx = []
            ov = None
            for c in range(nc):
                part = weighted(hd, c, m)
                ov = part if ov is None else ov + part
                if hd + 1 < nh:
                    mx.append(scores(hd + 1, c))
            outs.append(ov[:B_V] * (1.0 / ov[B_V:B_V + 1]))
        o_t = jnp.concatenate(outs, axis=0)
        act = (o_t.T * sg_ref[0].astype(F32)).astype(BF16)
        o_ref[0] = x_ref[0] + mod_ref[0][2:3] * _dot(act, wo_ref[...])

    pl.when(i < n_ctx_tiles)(lambda: attend(ctx_len))
    pl.when(i >= n_ctx_tiles)(lambda: attend(k_ref.shape[1]))


def _attn_b(q_t, kpad, v_t, sg, xc, wo_bf, mod, ctx_len, bsz):
    b, s, hw = kpad.shape
    d = xc.shape[2]
    tq = ROW_TILE
    n_ctx_tiles = ctx_len // tq
    body = functools.partial(_attn_b_body, ctx_len=ctx_len, n_ctx_tiles=n_ctx_tiles)
    row = lambda bb, i: (bb, i, 0)
    whole = lambda bb, i: (bb, 0, 0)
    return pl.pallas_call(
        body,
        out_shape=jax.ShapeDtypeStruct(xc.shape, F32),
        grid=(b, s // tq),
        in_specs=[pl.BlockSpec((1, hw, tq), lambda bb, i: (bb, 0, i)),
                  pl.BlockSpec((1, s, hw), whole),
                  pl.BlockSpec((1, v_t.shape[1], s), whole),
                  pl.BlockSpec((1, tq, sg.shape[2]), row),
                  pl.BlockSpec((1, tq, d), row),
                  pl.BlockSpec(wo_bf.shape, lambda bb, i: (0, 0)),
                  pl.BlockSpec((1, 3, d), lambda bb, i: (jnp.where(i < n_ctx_tiles, bsz, bb), 0, 0))],
        out_specs=pl.BlockSpec((1, tq, d), row),
        scratch_shapes=[pltpu.VMEM((2, s, tq), F32)],
        input_output_aliases={4: 0},
        compiler_params=_cparams(("parallel", "arbitrary")),
        name="attn_b",
    )(q_t, kpad, v_t, sg, xc, wo_bf, mod)


C_ROW_TILE = 256


def _proj_c_body(x_ref, modc_ref, modx_ref, g_ref, wdt_ref, dtb_ref, w_ref, cw_ref, cb_ref,
                 z_ref, xbc_ref, dt_ref, ht_ref, *, ctx_len, z_tiles):
    j = pl.program_id(1)
    s = x_ref.shape[1]
    tm = ROW_TILE

    @pl.when(j == 0)
    def _():
        for r in range(s // tm):
            mod = modc_ref[0] if r * tm < ctx_len else modx_ref[0]
            hh = _norm_mod(x_ref[0, r * tm:(r + 1) * tm, :], g_ref[...], mod)
            ht_ref[:, r * tm:(r + 1) * tm] = hh.T.astype(BF16)
        dt = _dot(wdt_ref[...], ht_ref[...]) + dtb_ref[...]
        dt_ref[0] = jnp.maximum(dt, 0.0) + jnp.log1p(jnp.exp(-jnp.abs(dt)))

    @pl.when(j < z_tiles)
    def _():
        z_ref[0] = _silu(_dot(w_ref[...], ht_ref[...])).astype(BF16)

    @pl.when(j >= z_tiles)
    def _():
        t = lax.broadcasted_iota(jnp.int32, (1, s), 1)
        has_prev = (t != 0) & (t != ctx_len)
        has_next = (t != ctx_len - 1) & (t != s - 1)
        half = w_ref.shape[0] // 2
        for r0 in (0, half):
            rs = slice(r0, r0 + half)
            u = _dot(w_ref[rs, :], ht_ref[...])
            up = jnp.where(has_prev, pltpu.roll(u, 1, 1), 0.0)
            un = jnp.where(has_next, pltpu.roll(u, s - 1, 1), 0.0)
            cw = cw_ref[rs, :]
            v = cw[:, 0:1] * up + cw[:, 1:2] * u + cw[:, 2:3] * un + cb_ref[rs, :]
            xbc_ref[0, rs, :] = _silu(v).astype(BF16)


def _proj_c(xc, mod, g, wt_bf, wdt_bf, dtb, cwt, cbt, ctx_len, bsz, inner):
    b, s, d = xc.shape
    tc = C_ROW_TILE
    nrows = wt_bf.shape[0]
    z_tiles = inner // tc
    n_tiles = nrows // tc
    conv_dim = nrows - inner
    ndt = wdt_bf.shape[0]
    body = functools.partial(_proj_c_body, ctx_len=ctx_len, z_tiles=z_tiles)
    const2 = lambda bb, j: (0, 0)
    return pl.pallas_call(
        body,
        out_shape=(jax.ShapeDtypeStruct((b, inner, s), BF16),
                   jax.ShapeDtypeStruct((b, conv_dim, s), BF16),
                   jax.ShapeDtypeStruct((b, ndt, s), F32)),
        grid=(b, n_tiles),
        in_specs=[pl.BlockSpec((1, s, d), lambda bb, j: (bb, 0, 0)),
                  pl.BlockSpec((1, 3, d), lambda bb, j: (bsz, 0, 0)),
                  pl.BlockSpec((1, 3, d), lambda bb, j: (bb, 0, 0)),
                  pl.BlockSpec((1, d), const2),
                  pl.BlockSpec((ndt, d), const2),
                  pl.BlockSpec((ndt, 1), const2),
                  pl.BlockSpec((tc, d), lambda bb, j: (j, 0)),
                  pl.BlockSpec((tc, 3), lambda bb, j: (jnp.maximum(j - z_tiles, 0), 0)),
                  pl.BlockSpec((tc, 1), lambda bb, j: (jnp.maximum(j - z_tiles, 0), 0))],
        out_specs=(pl.BlockSpec((1, tc, s), lambda bb, j: (bb, jnp.minimum(j, z_tiles - 1), 0)),
                   pl.BlockSpec((1, tc, s), lambda bb, j: (bb, jnp.maximum(j - z_tiles, 0), 0)),
                   pl.BlockSpec((1, ndt, s), lambda bb, j: (bb, 0, 0))),
        scratch_shapes=[pltpu.VMEM((d, s), BF16)],
        compiler_params=_cparams(("parallel", "arbitrary")),
        name="proj_c",
    )(xc, mod, mod, g, wdt_bf, dtb, wt_bf, cwt, cbt)


def _ssd_chunk(t, n_chunks, ctx_chunks):
    tb = t - n_chunks
    cb = jnp.where(tb < ctx_chunks, ctx_chunks - 1 - tb, n_chunks - 1 - (tb - ctx_chunks))
    return jnp.where(t < n_chunks, t, cb)


def _rep_rows(a, reps):
    r = a.shape[0]
    return jnp.broadcast_to(a[:, None, :], (r, reps, a.shape[1])).reshape(r * reps, a.shape[1])


def _ssd_body(z_ref, xbc_ref, dt_ref, dtn_ref, a_ref, dsk_ref, nw_ref, x_ref, wo_ref, mod_ref,
              o_ref, yf_ref, st_ref, yb_ref, pro_ref, *, n_chunks, ctx_chunks, heads, inner):
    t = pl.program_id(1)
    q = C_CHUNK
    hpg = heads // C_GROUPS
    gw = hpg * C_HEAD_DIM
    bwd = t >= n_chunks
    d = bwd.astype(jnp.int32)
    c = _ssd_chunk(t, n_chunks, ctx_chunks)
    ii = lax.broadcasted_iota(jnp.int32, (q, q), 0)
    jj = lax.broadcasted_iota(jnp.int32, (q, q), 1)

    def decay_terms(dt_blk_ref, step):
        rev = step >= n_chunks
        rows = pl.ds(pl.multiple_of(rev.astype(jnp.int32) * heads, heads), heads)
        dtd = dt_blk_ref[0, rows, :]
        a = dtd * a_ref[rows, :]
        tri = (ii <= jj).astype(BF16)
        a_hi, a_lo = _split2(a)
        a_lo2 = (a - a_hi.astype(F32) - a_lo.astype(F32)).astype(BF16)
        cum = _dot(a_hi, tri) + _dot(a_lo, tri) + _dot(a_lo2, tri)
        tot = jnp.broadcast_to(cum[:, q - 1:q], cum.shape)
        u = jnp.where(rev, tot - cum + a, cum)
        pro_ref[0] = dtd
        pro_ref[1] = u * LOG2E
        pro_ref[2] = jnp.exp(u)
        pro_ref[3] = dtd * jnp.exp(tot - u)
        pro_ref[4] = jnp.exp(tot)

    @pl.when(t == 0)
    def _():
        decay_terms(dt_ref, t)

    @pl.when((t == 0) | (t == n_chunks))
    def _():
        st_ref[...] = jnp.zeros_like(st_ref)

    dtd, u2, eu, dtdend, etot = (pro_ref[i] for i in range(5))
    decay_terms(dtn_ref, t + 1)
    sgn = 1 - 2 * d
    keep = (jj - ii) * sgn >= 0

    for g in range(C_GROUPS):
        r0 = g * gw
        hs = slice(g * hpg, (g + 1) * hpg)
        b_t = xbc_ref[0, inner + g * C_STATE: inner + (g + 1) * C_STATE, :]
        c_t = xbc_ref[0, inner + (C_GROUPS + g) * C_STATE: inner + (C_GROUPS + g + 1) * C_STATE, :]
        b_g = b_t.astype(F32).T.astype(BF16)
        cb_t = _dot(b_g, c_t)
        xs = xbc_ref[0, r0:r0 + gw, :].astype(F32)
        x_bf = (xs * _rep_rows(dtd[hs], C_HEAD_DIM)).astype(BF16)
        st = st_ref[r0:r0 + gw, :]
        y_off = _dot(st.astype(BF16), c_t) * _rep_rows(eu[hs], C_HEAD_DIM)
        st_ref[r0:r0 + gw, :] = (_rep_rows(etot[hs], C_HEAD_DIM) * st
                                 + _dot((xs * _rep_rows(dtdend[hs], C_HEAD_DIM)).astype(BF16), b_g))
        for hh in range(hpg):
            hd = g * hpg + hh
            r = jnp.broadcast_to(u2[hd:hd + 1, :], (q, q))
            e = jnp.where(keep, r - r.T, NEG)
            m_t = (jnp.exp2(e) * cb_t).astype(BF16)
            rr = slice(hh * C_HEAD_DIM, (hh + 1) * C_HEAD_DIM)
            yb_ref[r0 + hh * C_HEAD_DIM:r0 + (hh + 1) * C_HEAD_DIM, :] = _dot(x_bf[rr], m_t) + y_off[rr]

    @pl.when(jnp.logical_not(bwd))
    def _():
        yf_ref[c] = yb_ref[...] + dsk_ref[...] * xbc_ref[0, 0:inner, :].astype(F32)

    @pl.when(bwd)
    def _():
        acts = []
        for g in range(C_GROUPS):
            r0 = g * gw
            rs = slice(r0, r0 + gw)
            y = yf_ref[c, rs, :] + yb_ref[rs, :]
            y = y * z_ref[0, rs, :].astype(F32)
            ms = jnp.mean(y * y, axis=0, keepdims=True)
            y = y * lax.rsqrt(ms + EPS) * nw_ref[rs, :]
            acts.append(y.T.astype(BF16))
        o_ref[0] = x_ref[0] + mod_ref[0][2:3] * _dot(jnp.concatenate(acts, axis=1), wo_ref[...])


def _ssd(z_t, xbc_t, dt_t, a_tab, dsk_tab, nw_tab, xc, wo_bf, mod, ctx_len, bsz):
    b, inner, s = z_t.shape
    d = xc.shape[2]
    conv_dim = xbc_t.shape[1]
    heads = inner // C_HEAD_DIM
    n_chunks = s // C_CHUNK
    ctx_chunks = ctx_len // C_CHUNK
    body = functools.partial(_ssd_body, n_chunks=n_chunks, ctx_chunks=ctx_chunks, heads=heads,
                             inner=inner)
    cmap = lambda bb, t: (bb, 0, _ssd_chunk(t, n_chunks, ctx_chunks))
    nmap = lambda bb, t: (bb, 0, _ssd_chunk(jnp.minimum(t + 1, 2 * n_chunks - 1), n_chunks, ctx_chunks))
    out_chunk = lambda t: _ssd_chunk(jnp.maximum(t, n_chunks), n_chunks, ctx_chunks)
    omap = lambda bb, t: (bb, out_chunk(t), 0)
    modrow = lambda bb, t: (jnp.where(out_chunk(t) < ctx_chunks, bsz, bb), 0, 0)
    const2 = lambda bb, t: (0, 0)
    return pl.pallas_call(
        body,
        out_shape=jax.ShapeDtypeStruct(xc.shape, F32),
        grid=(b, 2 * n_chunks),
        in_specs=[pl.BlockSpec((1, inner, C_CHUNK), cmap),
                  pl.BlockSpec((1, conv_dim, C_CHUNK), cmap),
                  pl.BlockSpec((1, 2 * heads, C_CHUNK), cmap),
                  pl.BlockSpec((1, 2 * heads, C_CHUNK), nmap),
                  pl.BlockSpec((2 * heads, LANES), const2),
                  pl.BlockSpec((inner, LANES), const2),
                  pl.BlockSpec((inner, LANES), const2),
                  pl.BlockSpec((1, C_CHUNK, d), omap),
                  pl.BlockSpec(wo_bf.shape, const2),
                  pl.BlockSpec((1, 3, d), modrow)],
        out_specs=pl.BlockSpec((1, C_CHUNK, d), omap),
        scratch_shapes=[pltpu.VMEM((n_chunks, inner, C_CHUNK), F32),
                        pltpu.VMEM((inner, C_STATE), F32),
                        pltpu.VMEM((inner, C_CHUNK), F32),
                        pltpu.VMEM((5, heads, C_CHUNK), F32)],
        input_output_aliases={7: 0},
        compiler_params=_cparams(("parallel", "arbitrary")),
        name="ssd_scan",
    )(z_t, xbc_t, dt_t, dt_t, a_tab, dsk_tab, nw_tab, xc, wo_bf, mod)


def _lanes(v):
    return jnp.broadcast_to(v.astype(F32)[:, None], (v.shape[0], LANES))


def kernel(x, c, ctx, c_ctx, ada_w, ada_b, norm_g, final_g, a_w_in, a_sink, a_w_out, b_w_in, b_q_norm, b_w_uq, b_kv_norm, b_w_ukv, b_w_out, c_w_in, c_conv_w, c_conv_b, c_dt_bias, c_a_log, c_d, c_norm, c_w_out):
    bsz, t_len, d = x.shape
    ctx_len = ctx.shape[1]
    depth = ada_w.shape[0]
    assert ctx_len % ROW_TILE == 0 and t_len % ROW_TILE == 0 and t_len % GRID_W == 0
    n_ctx_tiles = ctx_len // ROW_TILE
    rows = t_len // GRID_W

    r_pad = -(-(bsz + 1) // 8) * 8
    cond = jnp.concatenate([c, c_ctx[None], jnp.zeros((r_pad - bsz - 1, d), F32)], axis=0)
    mods = _adaln_all(cond, ada_w, ada_b).reshape(depth, r_pad, 3, d)

    lane = jnp.arange(LANES)
    a_scale = A_HEAD_DIM ** -0.5 * LOG2E
    a_lane_dim = lane % A_HEAD_DIM
    a_tabs = (_rope_tables_t(rows, A_HEAD_DIM, ctx_len, a_scale)
              + _rope_tables(rows, A_HEAD_DIM, ctx_len, a_lane_dim, 1.0))
    b_scale = (B_NOPE + B_ROPE) ** -0.5 * LOG2E
    b_lane_dim = jnp.where((lane >= B_NOPE) & (lane < B_NOPE + B_ROPE), lane - B_NOPE, -1)
    b_tabs = (_rope_tables_t(rows, B_ROPE, ctx_len, b_scale)
              + _rope_tables(rows, B_ROPE, ctx_len, b_lane_dim, 1.0))

    for i in range(depth):
        kind = i % N_MIXERS
        j = i // N_MIXERS
        last = i == depth - 1
        mod = mods[i]
        g = norm_g[i][None]
        if kind == 0:
            w = a_w_in[j]
            qw, kw = A_HEADS * A_HEAD_DIM, A_KV_HEADS * A_HEAD_DIM
            w_tok = jnp.concatenate([w[:, qw:qw + kw], w[:, qw + 2 * kw:]], axis=1)
            xs = (ctx, x) if i == 0 else (xc,)
            q_t, k, v_t, sg = _proj_a(xs, mod, g, w[:, :qw].T.astype(BF16),
                                      w[:, qw + kw:qw + 2 * kw].T.astype(BF16), w_tok.astype(BF16),
                                      a_tabs, n_ctx_tiles, bsz)
            xc = _attn_a(q_t, k, v_t, sg, a_sink[j].astype(F32), xs, a_w_out[j].astype(BF16), mod,
                         ctx_len, bsz, final_g=final_g[None] if last else None)
        elif kind == 1:
            w = b_w_in[j]
            q_lora = b_q_norm.shape[1]
            kv_lora = b_kv_norm.shape[1]
            k0 = q_lora + kv_lora
            w_pad = jnp.concatenate(
                [w[:, :k0], jnp.zeros((d, B_NOPE), F32), w[:, k0:k0 + B_ROPE],
                 jnp.zeros((d, LANES - B_NOPE - B_ROPE), F32), w[:, k0 + B_ROPE:]], axis=1)
            wuq = b_w_uq[j].reshape(q_lora, B_HEADS, B_NOPE + B_ROPE)
            wuq = jnp.pad(wuq, ((0, 0), (0, 0), (0, LANES - B_NOPE - B_ROPE))).reshape(q_lora, B_HEADS * LANES)
            wukv = b_w_ukv[j].reshape(kv_lora, B_HEADS, B_NOPE + B_V)
            wkn = wukv[:, :, :B_NOPE].reshape(kv_lora, B_HEADS * B_NOPE)
            wv = wukv[:, :, B_NOPE:].reshape(kv_lora, B_HEADS * B_V)
            q_t, kp, v_t, sg = _proj_b(xc, mod, g, w_pad.astype(BF16), b_q_norm[j][None],
                                       wuq.T.astype(BF16), b_kv_norm[j][None], wkn.astype(BF16),
                                       wv.T.astype(BF16), b_tabs, b_scale, n_ctx_tiles, bsz)
            xc = _attn_b(q_t, kp, v_t, sg, xc, b_w_out[j].astype(BF16), mod, ctx_len, bsz)
        else:
            inner = c_norm.shape[1]
            heads = inner // C_HEAD_DIM
            conv_dim = c_conv_w.shape[2]
            w = c_w_in[j]
            wt = w[:, :inner + conv_dim].T.astype(BF16)
            wdt = w[:, inner + conv_dim:].T.astype(BF16)
            z_t, xbc_t, dt_t = _proj_c(xc, mod, g, wt, wdt, c_dt_bias[j].reshape(2 * heads, 1),
                                       c_conv_w[j].T, c_conv_b[j][:, None], ctx_len, bsz, inner)
            a_tab = _lanes(-jnp.exp(c_a_log[j].astype(F32)).reshape(2 * heads))
            dsk_tab = _lanes(jnp.repeat(c_d[j], C_HEAD_DIM))
            xc = _ssd(z_t, xbc_t, dt_t, a_tab, dsk_tab, _lanes(c_norm[j]), xc, c_w_out[j].astype(BF16), mod,
                      ctx_len, bsz)
    assert (depth - 1) % N_MIXERS == 0
    return xc
```

```python
import functools
import math

import jax
import jax.numpy as jnp
from jax import lax
from jax.experimental import pallas as pl
from jax.experimental.pallas import tpu as pltpu

F32 = jnp.float32
BF16 = jnp.bfloat16

GRID_W = 64
N_MIXERS = 3
ROPE_BASE = 10000.0
EPS = 1e-6
BLOCK = 128
WINDOW = 128

A_HEADS = 16
A_KV_HEADS = 4
A_GROUP = A_HEADS // A_KV_HEADS
A_HEAD_DIM = 64

B_HEADS = 16
B_NOPE = 64
B_ROPE = 32
B_V = 64

C_HEAD_DIM = 64
C_GROUPS = 4
C_STATE = 128
C_CHUNK = 128

LANES = 128
BF16_ROWS = 16
ROW_TILE = 256
VMEM_LIMIT = 56 * 1024 * 1024
NEG = -1e30
LOG2E = math.log2(math.e)


def _cparams(sem):
    return pltpu.CompilerParams(dimension_semantics=sem, vmem_limit_bytes=VMEM_LIMIT)


def _dot(a, b):
    return jnp.dot(a, b, preferred_element_type=F32)


def _dot_nt(a, b):
    return lax.dot_general(a, b, (((1,), (1,)), ((), ())), preferred_element_type=F32)


def _silu(v):
    return v / (1.0 + jnp.exp(-v))


def _split2(a):
    hi = a.astype(BF16)
    lo = (a - hi.astype(F32)).astype(BF16)
    return hi, lo


def _dot_f32(a, b):
    a_hi, a_lo = _split2(a)
    b_hi, b_lo = _split2(b)
    return _dot(a_hi, b_hi) + _dot(a_hi, b_lo) + _dot(a_lo, b_hi)


def _norm_mod(x, g, mod):
    ms = jnp.mean(x * x, axis=-1, keepdims=True)
    y = x * lax.rsqrt(ms + EPS) * g
    return y * (1.0 + mod[1:2]) + mod[0:1]


def _with_ones(v_t):
    return jnp.concatenate([v_t, jnp.ones((BF16_ROWS, v_t.shape[1]), BF16)], axis=0)


def _adaln_body(cond_ref, w_ref, b_ref, o_ref):
    o_ref[0] = _dot_f32(_silu(cond_ref[...]), w_ref[0]) + b_ref[0]


def _adaln_all(cond, ada_w, ada_b):
    depth, d, _ = ada_w.shape
    r = cond.shape[0]
    return pl.pallas_call(
        _adaln_body,
        out_shape=jax.ShapeDtypeStruct((depth, r, 3 * d), F32),
        grid=(depth, 3),
        in_specs=[pl.BlockSpec((r, d), lambda i, j: (0, 0)),
                  pl.BlockSpec((1, d, d), lambda i, j: (i, 0, j)),
                  pl.BlockSpec((1, 1, d), lambda i, j: (i, 0, j))],
        out_specs=pl.BlockSpec((1, r, d), lambda i, j: (i, 0, j)),
        compiler_params=_cparams(("arbitrary", "arbitrary")),
        name="adaln_mod",
    )(cond, ada_w, ada_b.reshape(depth, 1, 3 * d))


def _rope_tables(rows, dim, ctx_len, lane_of_dim, scale):
    row = jnp.repeat(jnp.arange(rows), GRID_W).astype(F32)
    col = (jnp.arange(rows * GRID_W) % GRID_W).astype(F32)
    nf = dim // 4
    inv = ROPE_BASE ** (-jnp.arange(nf, dtype=F32) / nf)
    ar = row[:, None] * inv
    ac = col[:, None] * inv
    ang = jnp.concatenate([ar, ar, ac, ac], axis=-1)
    cos, sin = jnp.cos(ang), jnp.sin(ang)
    sign = jnp.where((jnp.arange(dim) // nf) % 2 == 0, -1.0, 1.0).astype(F32)
    sin = sin * sign
    idx = jnp.asarray(lane_of_dim)
    valid = (idx >= 0)[None, :]
    cos_l = jnp.where(valid, cos[:, jnp.maximum(idx, 0)], 1.0)
    sin_l = jnp.where(valid, sin[:, jnp.maximum(idx, 0)], 0.0)
    cos_l = jnp.concatenate([jnp.ones((ctx_len, LANES), F32), cos_l], axis=0)
    sin_l = jnp.concatenate([jnp.zeros((ctx_len, LANES), F32), sin_l], axis=0)
    return cos_l * scale, sin_l * scale


def _rope128(u, cos, sin, quarter, even):
    fwd = pltpu.roll(u, quarter, 1)
    bwd = pltpu.roll(u, LANES - quarter, 1)
    return u * cos + jnp.where(even, bwd, fwd) * sin


def _rope_tables_t(rows, dim, ctx_len, scale):
    row = jnp.repeat(jnp.arange(rows), GRID_W).astype(F32)
    col = (jnp.arange(rows * GRID_W) % GRID_W).astype(F32)
    nf = dim // 4
    inv = ROPE_BASE ** (-jnp.arange(nf, dtype=F32) / nf)
    ar = inv[:, None] * row[None, :]
    ac = inv[:, None] * col[None, :]
    ang = jnp.concatenate([ar, ar, ac, ac], axis=0)
    cos = jnp.concatenate([jnp.ones((dim, ctx_len), F32), jnp.cos(ang)], axis=1)
    sin = jnp.concatenate([jnp.zeros((dim, ctx_len), F32), jnp.sin(ang)], axis=1)
    return cos * scale, sin * scale


def _rope_rows(u, cos_t, sin_t):
    qd = u.shape[0] // 4
    r1, r2, c1, c2 = (u[i * qd:(i + 1) * qd] for i in range(4))
    rot = jnp.concatenate([-r2, r1, -c2, c1], axis=0)
    return u * cos_t + rot * sin_t


def _step_rows(x_refs, is_ctx):
    if len(x_refs) == 1:
        return x_refs[0][0]
    return jnp.where(is_ctx, x_refs[0][0], x_refs[1][0])


def _row_specs(xs, rows, n_ctx_steps, first=0):
    d = xs[0].shape[2]
    if len(xs) == 1:
        return [pl.BlockSpec((1, rows, d), lambda bb, i: (bb, i + first, 0))]
    return [pl.BlockSpec((1, rows, d), lambda bb, i: (bb, jnp.minimum(i + first, n_ctx_steps - 1), 0)),
            pl.BlockSpec((1, rows, d), lambda bb, i: (bb, jnp.maximum(i + first - n_ctx_steps, 0), 0))]


def _proj_a_body(*refs, n_x, n_ctx_tiles):
    x_refs = refs[:n_x]
    (mod_ref, g_ref, wqt_ref, wvt_ref, w_ref, cqt_ref, sqt_ref, ck_ref, sk_ref,
     qt_ref, k_ref, vt_ref, sg_ref) = refs[n_x:]
    tm = x_refs[0].shape[1]
    x_rows = _step_rows(x_refs, pl.program_id(1) < n_ctx_tiles)
    h = _norm_mod(x_rows, g_ref[...], mod_ref[0]).astype(BF16)
    kw = A_KV_HEADS * A_HEAD_DIM
    qt = _dot_nt(wqt_ref[...], h)
    cqt, sqt = cqt_ref[...], sqt_ref[...]
    for hd in range(A_HEADS):
        rs = slice(hd * A_HEAD_DIM, (hd + 1) * A_HEAD_DIM)
        qt_ref[0, rs, :] = _rope_rows(qt[rs], cqt, sqt).astype(BF16)
    vt_ref[0] = _dot_nt(wvt_ref[...], h).astype(BF16)
    y = _dot(h, w_ref[...])
    lane = lax.broadcasted_iota(jnp.int32, (tm, LANES), 1)
    even = ((lane % A_HEAD_DIM) // (A_HEAD_DIM // 4)) % 2 == 0
    ck, sk = ck_ref[...], sk_ref[...]
    for j in range(kw // LANES):
        kg = _rope128(y[:, j * LANES:(j + 1) * LANES], ck, sk, A_HEAD_DIM // 4, even)
        k_ref[0, :, j * LANES:(j + 1) * LANES] = kg.astype(BF16)
    sg_ref[0] = _silu(y[:, kw:]).astype(BF16)


def _proj_a(xs, mod, g, wqt_bf, wvt_bf, w_bf, tabs, n_ctx_tiles, bsz):
    b, _, d = xs[0].shape
    s = sum(a.shape[1] for a in xs)
    tm = ROW_TILE
    qw = A_HEADS * A_HEAD_DIM
    kw = A_KV_HEADS * A_HEAD_DIM
    row = lambda bb, i: (bb, i, 0)
    modrow = lambda bb, i: (jnp.where(i < n_ctx_tiles, bsz, bb), 0, 0)
    const2 = lambda bb, i: (0, 0)
    full = lambda a: pl.BlockSpec(a.shape, const2)
    tab = pl.BlockSpec((tm, LANES), lambda bb, i: (i, 0))
    tab_t = pl.BlockSpec((A_HEAD_DIM, tm), lambda bb, i: (0, i))
    wide = jax.ShapeDtypeStruct((b, s, qw), BF16)
    return pl.pallas_call(
        functools.partial(_proj_a_body, n_x=len(xs), n_ctx_tiles=n_ctx_tiles),
        out_shape=(jax.ShapeDtypeStruct((b, qw, s), BF16), jax.ShapeDtypeStruct((b, s, kw), BF16),
                   jax.ShapeDtypeStruct((b, kw, s), BF16), wide),
        grid=(b, s // tm),
        in_specs=[*_row_specs(xs, tm, n_ctx_tiles),
                  pl.BlockSpec((1, 3, d), modrow),
                  pl.BlockSpec((1, d), const2),
                  full(wqt_bf), full(wvt_bf), full(w_bf),
                  tab_t, tab_t, tab, tab],
        out_specs=(pl.BlockSpec((1, qw, tm), lambda bb, i: (bb, 0, i)), pl.BlockSpec((1, tm, kw), row),
                   pl.BlockSpec((1, kw, tm), lambda bb, i: (bb, 0, i)), pl.BlockSpec((1, tm, qw), row)),
        compiler_params=_cparams(("parallel", "arbitrary")),
        name="proj_a",
    )(*xs, mod, g, wqt_bf, wvt_bf, w_bf, *tabs)


A_BLOCKS_PER_STEP = 2


def _attn_a_body(sink_ref, qt_ref, kp_ref, ko_ref, kn_ref, kc_ref, vp_ref, vo_ref, vn_ref, vc_ref,
                 sg_ref, *rest, n_x, first_pair, ctx_pairs, lat_pairs, final):
    x_refs, rest = rest[:n_x], rest[n_x:]
    if final:
        wo_ref, mod_ref, fg_ref, o_ref, s_ref = rest
    else:
        wo_ref, mod_ref, o_ref, s_ref = rest
    jp = pl.program_id(1) + first_pair - ctx_pairs
    nq = BLOCK
    nsub = A_BLOCKS_PER_STEP
    cols = A_GROUP * nq
    colq = lax.broadcasted_iota(jnp.int32, (1, cols), 1)
    nctx = kc_ref.shape[1]

    def run(local):
        if local:
            kk = lax.broadcasted_iota(jnp.int32, (nq, cols), 0)
            qi = lax.broadcasted_iota(jnp.int32, (nq, cols), 1) % nq
            far = 4 * nq
            no_prev = jnp.where(jp > 0, 0, far)
            no_next = jnp.where(jp < lat_pairs - 1, 0, far)
            every = jnp.full((nq, cols), True)
            ok = [jnp.concatenate([kk - qi >= no_prev, every, kk <= qi], axis=0),
                  jnp.concatenate([kk >= qi, every, qi - kk >= no_next], axis=0)]

        def k_local(sub, hl):
            own = [ko_ref[0, 0:nq, hl], ko_ref[0, nq:2 * nq, hl]]
            return [kp_ref[0, :, hl]] + own if sub == 0 else own + [kn_ref[0, :, hl]]

        def v_local(sub, hr):
            own = [vo_ref[0, hr, 0:nq], vo_ref[0, hr, nq:2 * nq]]
            return [vp_ref[0, hr, :]] + own if sub == 0 else own + [vn_ref[0, hr, :]]

        units = [(sub, h) for sub in range(nsub) for h in range(A_KV_HEADS)]

        def sink_row(h):
            row = jnp.zeros((1, cols), F32)
            for g in range(A_GROUP):
                row = jnp.where(colq // nq == g, sink_ref[h * A_GROUP + g], row)
            return row * LOG2E

        def scores(u, part):
            sub, h = units[u]
            q4 = jnp.concatenate(
                [qt_ref[0, (h * A_GROUP + g) * A_HEAD_DIM:(h * A_GROUP + g + 1) * A_HEAD_DIM, sub * nq:(sub + 1) * nq]
                 for g in range(A_GROUP)], axis=1)
            zero = jnp.zeros_like(q4)
            rhs = jnp.concatenate([q4, zero] if h % 2 == 0 else [zero, q4], axis=0)
            hl = slice((h // 2) * LANES, (h // 2 + 1) * LANES)
            if part == 0:
                s_t = _dot(kc_ref[0, :, hl], rhs)
                s_ref[u % 2, 0:nctx, :] = s_t
            else:
                s_t = jnp.where(ok[sub], _dot(jnp.concatenate(k_local(sub, hl), axis=0), rhs), NEG)
                s_ref[u % 2, nctx:nctx + 3 * nq, :] = s_t
            return jnp.max(s_t, axis=0, keepdims=True)

        def weighted(u, part, m):
            sub, h = units[u]
            hr = slice(h * A_HEAD_DIM, (h + 1) * A_HEAD_DIM)
            if part == 0:
                v1 = _with_ones(vc_ref[0, hr, :])
                p = jnp.exp2(s_ref[u % 2, 0:nctx, :] - m)
            else:
                v1 = _with_ones(jnp.concatenate(v_local(sub, hr), axis=1))
                p = jnp.exp2(s_ref[u % 2, nctx:nctx + 3 * nq, :] - m)
            return _dot(v1, p.astype(BF16))

        parts = (0, 1) if local else (0,)
        acts = [[] for _ in range(nsub)]
        mx = [scores(0, part) for part in parts]
        for u, (sub, h) in enumerate(units):
            snk = sink_row(h)
            m = functools.reduce(jnp.maximum, mx + [snk])
            mx = []
            ov = None
            for part in parts:
                piece = weighted(u, part, m)
                ov = piece if ov is None else ov + piece
                if u + 1 < len(units):
                    mx.append(scores(u + 1, part))
            den = ov[A_HEAD_DIM:A_HEAD_DIM + 1] + jnp.exp2(snk - m)
            out = ov[:A_HEAD_DIM] * (1.0 / den)
            c0 = h * A_GROUP * A_HEAD_DIM
            for pair in range(A_GROUP // 2):
                pc = slice(c0 + pair * LANES, c0 + (pair + 1) * LANES)
                two = jnp.concatenate([out[:, (2 * pair) * nq:(2 * pair + 1) * nq],
                                       out[:, (2 * pair + 1) * nq:(2 * pair + 2) * nq]], axis=0).T
                gate = sg_ref[0, sub * nq:(sub + 1) * nq, pc].astype(F32)
                acts[sub].append((two * gate).astype(BF16))
        act = jnp.concatenate([jnp.concatenate(a, axis=1) for a in acts], axis=0)
        xn = _step_rows(x_refs, jp < 0) + mod_ref[0][2:3] * _dot(act, wo_ref[...])
        if final:
            xn = xn * lax.rsqrt(jnp.mean(xn * xn, axis=-1, keepdims=True) + EPS) * fg_ref[...]
        o_ref[0] = xn

    if first_pair >= ctx_pairs:
        run(True)
    else:
        pl.when(jp >= 0)(lambda: run(True))
        pl.when(jp < 0)(lambda: run(False))


def _attn_a(q_t, k, v_t, sg, sink, xs, wo_bf, mod, ctx_len, bsz, final_g=None):
    b, qw, s = q_t.shape
    d = xs[0].shape[2]
    kw = v_t.shape[1]
    final = final_g is not None
    rows = A_BLOCKS_PER_STEP * BLOCK
    assert ctx_len % rows == 0 and (s - ctx_len) % rows == 0
    ctx_blocks = ctx_len // BLOCK
    lat_blocks = (s - ctx_len) // BLOCK
    ctx_pairs = ctx_len // rows
    lat_pairs = (s - ctx_len) // rows
    first_pair = ctx_pairs if final else 0
    npair = s // rows - first_pair

    def lat(i):
        return jnp.maximum(i + first_pair - ctx_pairs, 0)

    def prev_blk(i):
        return ctx_blocks + jnp.maximum(A_BLOCKS_PER_STEP * lat(i) - 1, 0)

    def next_blk(i):
        return ctx_blocks + jnp.minimum(A_BLOCKS_PER_STEP * (lat(i) + 1), lat_blocks - 1)

    own = lambda bb, i: (bb, i + first_pair, 0)
    own_t = lambda bb, i: (bb, 0, i + first_pair)
    body = functools.partial(_attn_a_body, n_x=len(xs), first_pair=first_pair, ctx_pairs=ctx_pairs,
                             lat_pairs=lat_pairs, final=final)
    const2 = lambda bb, i: (0, 0)
    modrow = lambda bb, i: (jnp.where(i + first_pair < ctx_pairs, bsz, bb), 0, 0)
    extra_in, extra_specs = ((final_g,), (pl.BlockSpec((1, d), const2),)) if final else ((), ())
    return pl.pallas_call(
        body,
        out_shape=jax.ShapeDtypeStruct((b, npair * rows, d), F32),
        grid=(b, npair),
        in_specs=[pl.BlockSpec(memory_space=pltpu.SMEM),
                  pl.BlockSpec((1, qw, rows), own_t),
                  pl.BlockSpec((1, BLOCK, kw), lambda bb, i: (bb, prev_blk(i), 0)),
                  pl.BlockSpec((1, rows, kw), own),
                  pl.BlockSpec((1, BLOCK, kw), lambda bb, i: (bb, next_blk(i), 0)),
                  pl.BlockSpec((1, ctx_len, kw), lambda bb, i: (bb, 0, 0)),
                  pl.BlockSpec((1, kw, BLOCK), lambda bb, i: (bb, 0, prev_blk(i))),
                  pl.BlockSpec((1, kw, rows), own_t),
                  pl.BlockSpec((1, kw, BLOCK), lambda bb, i: (bb, 0, next_blk(i))),
                  pl.BlockSpec((1, kw, ctx_len), lambda bb, i: (bb, 0, 0)),
                  pl.BlockSpec((1, rows, qw), own),
                  *_row_specs(xs, rows, ctx_pairs, first_pair),
                  pl.BlockSpec(wo_bf.shape, const2),
                  pl.BlockSpec((1, 3, d), modrow),
                  *extra_specs],
        out_specs=pl.BlockSpec((1, rows, d), lambda bb, i: (bb, i, 0)),
        scratch_shapes=[pltpu.VMEM((2, ctx_len + 3 * BLOCK, A_GROUP * BLOCK), F32)],
        input_output_aliases={11: 0} if (len(xs) == 1 and not final) else {},
        compiler_params=_cparams(("parallel", "arbitrary")),
        name="attn_a",
    )(sink, q_t, k, k, k, k, v_t, v_t, v_t, v_t, sg, *xs, wo_bf, mod, *extra_in)


def _proj_b_body(x_ref, mod_ref, g_ref, w_ref, qn_ref, wuqt_ref, kvn_ref, wkn_ref, wvt_ref,
                 cqt_ref, sqt_ref, ck_ref, sk_ref, qt_ref, k_ref, vt_ref, sg_ref, *, q_scale):
    tm = x_ref.shape[1]
    q_lora = qn_ref.shape[1]
    kv_lora = kvn_ref.shape[1]
    h = _norm_mod(x_ref[0], g_ref[...], mod_ref[0]).astype(BF16)
    y = _dot(h, w_ref[...])
    lane = lax.broadcasted_iota(jnp.int32, (tm, LANES), 1)
    even = (((lane - B_NOPE) // (B_ROPE // 4)) % 2 == 0)
    low = lane < B_NOPE

    def rms(u, gain):
        return u * lax.rsqrt(jnp.mean(u * u, axis=-1, keepdims=True) + EPS) * gain

    cq = rms(y[:, :q_lora], qn_ref[...]).astype(BF16)
    ckv = rms(y[:, q_lora:q_lora + kv_lora], kvn_ref[...]).astype(BF16)
    krg = _rope128(y[:, q_lora + kv_lora:q_lora + kv_lora + LANES], ck_ref[...], sk_ref[...],
                   B_ROPE // 4, even)
    qt = _dot_nt(wuqt_ref[...], cq)
    cqt, sqt = cqt_ref[...], sqt_ref[...]
    for hd in range(B_HEADS):
        r0 = hd * LANES
        qt_ref[0, r0:r0 + B_NOPE, :] = (qt[r0:r0 + B_NOPE] * q_scale).astype(BF16)
        qt_ref[0, r0 + B_NOPE:r0 + B_NOPE + B_ROPE, :] = _rope_rows(
            qt[r0 + B_NOPE:r0 + B_NOPE + B_ROPE], cqt, sqt).astype(BF16)
        qt_ref[0, r0 + B_NOPE + B_ROPE:r0 + LANES, :] = qt[r0 + B_NOPE + B_ROPE:r0 + LANES].astype(BF16)
    vt_ref[0] = _dot_nt(wvt_ref[...], ckv).astype(BF16)
    kn = _dot(ckv, wkn_ref[...])
    for hd in range(B_HEADS):
        grp = kn[:, (hd // 2) * LANES:(hd // 2 + 1) * LANES]
        if hd % 2 == 1:
            grp = pltpu.roll(grp, B_NOPE, 1)
        k_ref[0, :, hd * LANES:(hd + 1) * LANES] = jnp.where(low, grp, krg).astype(BF16)
    sg_ref[0] = _silu(y[:, q_lora + kv_lora + LANES:]).astype(BF16)


def _proj_b(xc, mod, g, w_bf, qn, wuqt_bf, kvn, wkn_bf, wvt_bf, tabs, q_scale, n_ctx_tiles, bsz):
    b, s, d = xc.shape
    tm = ROW_TILE
    hw = B_HEADS * LANES
    gw = B_HEADS * B_V
    row = lambda bb, i: (bb, i, 0)
    modrow = lambda bb, i: (jnp.where(i < n_ctx_tiles, bsz, bb), 0, 0)
    const2 = lambda bb, i: (0, 0)
    tab = pl.BlockSpec((tm, LANES), lambda bb, i: (i, 0))
    tab_t = pl.BlockSpec((B_ROPE, tm), lambda bb, i: (0, i))
    big = jax.ShapeDtypeStruct((b, s, hw), BF16)
    full = lambda a: pl.BlockSpec(a.shape, const2)
    return pl.pallas_call(
        functools.partial(_proj_b_body, q_scale=q_scale),
        out_shape=(jax.ShapeDtypeStruct((b, hw, s), BF16), big, jax.ShapeDtypeStruct((b, gw, s), BF16), jax.ShapeDtypeStruct((b, s, gw), BF16)),
        grid=(b, s // tm),
        in_specs=[pl.BlockSpec((1, tm, d), row),
                  pl.BlockSpec((1, 3, d), modrow),
                  pl.BlockSpec((1, d), const2),
                  full(w_bf), full(qn), full(wuqt_bf), full(kvn), full(wkn_bf), full(wvt_bf),
                  tab_t, tab_t, tab, tab],
        out_specs=(pl.BlockSpec((1, hw, tm), lambda bb, i: (bb, 0, i)), pl.BlockSpec((1, tm, hw), row),
                   pl.BlockSpec((1, gw, tm), lambda bb, i: (bb, 0, i)), pl.BlockSpec((1, tm, gw), row)),
        compiler_params=_cparams(("parallel", "arbitrary")),
        name="proj_b",
    )(xc, mod, g, w_bf, qn, wuqt_bf, kvn, wkn_bf, wvt_bf, *tabs)


B_KEY_CHUNK = 128


def _attn_b_body(qt_ref, k_ref, vt_ref, sg_ref, x_ref, wo_ref, mod_ref, o_ref, s_ref,
                 *, ctx_len, n_ctx_tiles):
    i = pl.program_id(1)
    nh = B_HEADS
    kc = B_KEY_CHUNK

    def attend(nk):
        nc = nk // kc

        def scores(hd, c):
            sl = slice(hd * LANES, (hd + 1) * LANES)
            s_t = _dot(k_ref[0, c * kc:(c + 1) * kc, sl], qt_ref[0, sl, :])
            s_ref[hd % 2, c * kc:(c + 1) * kc, :] = s_t
            return jnp.max(s_t, axis=0, keepdims=True)

        def weighted(hd, c, m):
            p = jnp.exp2(s_ref[hd % 2, c * kc:(c + 1) * kc, :] - m).astype(BF16)
            v1 = _with_ones(vt_ref[0, hd * B_V:(hd + 1) * B_V, c * kc:(c + 1) * kc])
            return _dot(v1, p)

        outs = []
        mx = [scores(0, c) for c in range(nc)]
        for hd in range(nh):
            m = functools.reduce(jnp.maximum, mx)
            mx = []
            ov = None
            for c in range(nc):
                part = weighted(hd, c, m)
                ov = part if ov is None else ov + part
                if hd + 1 < nh:
                    mx.append(scores(hd + 1, c))
            outs.append(ov[:B_V] * (1.0 / ov[B_V:B_V + 1]))
        o_t = jnp.concatenate(outs, axis=0)
        act = (o_t.T * sg_ref[0].astype(F32)).astype(BF16)
        o_ref[0] = x_ref[0] + mod_ref[0][2:3] * _dot(act, wo_ref[...])

    pl.when(i < n_ctx_tiles)(lambda: attend(ctx_len))
    pl.when(i >= n_ctx_tiles)(lambda: attend(k_ref.shape[1]))


def _attn_b(q_t, kpad, v_t, sg, xc, wo_bf, mod, ctx_len, bsz):
    b, s, hw = kpad.shape
    d = xc.shape[2]
    tq = ROW_TILE
    n_ctx_tiles = ctx_len // tq
    body = functools.partial(_attn_b_body, ctx_len=ctx_len, n_ctx_tiles=n_ctx_tiles)
    row = lambda bb, i: (bb, i, 0)
    whole = lambda bb, i: (bb, 0, 0)
    return pl.pallas_call(
        body,
        out_shape=jax.ShapeDtypeStruct(xc.shape, F32),
        grid=(b, s // tq),
        in_specs=[pl.BlockSpec((1, hw, tq), lambda bb, i: (bb, 0, i)),
                  pl.BlockSpec((1, s, hw), whole),
                  pl.BlockSpec((1, v_t.shape[1], s), whole),
                  pl.BlockSpec((1, tq, sg.shape[2]), row),
                  pl.BlockSpec((1, tq, d), row),
                  pl.BlockSpec(wo_bf.shape, lambda bb, i: (0, 0)),
                  pl.BlockSpec((1, 3, d), lambda bb, i: (jnp.where(i < n_ctx_tiles, bsz, bb), 0, 0))],
        out_specs=pl.BlockSpec((1, tq, d), row),
        scratch_shapes=[pltpu.VMEM((2, s, tq), F32)],
        input_output_aliases={4: 0},
        compiler_params=_cparams(("parallel", "arbitrary")),
        name="attn_b",
    )(q_t, kpad, v_t, sg, xc, wo_bf, mod)


C_ROW_TILE = 256


def _proj_c_body(x_ref, modc_ref, modx_ref, g_ref, wdt_ref, dtb_ref, w_ref, cw_ref, cb_ref,
                 z_ref, xbc_ref, dt_ref, ht_ref, *, ctx_len, z_tiles):
    j = pl.program_id(1)
    s = x_ref.shape[1]
    tm = ROW_TILE

    @pl.when(j == 0)
    def _():
        for r in range(s // tm):
            mod = modc_ref[0] if r * tm < ctx_len else modx_ref[0]
            hh = _norm_mod(x_ref[0, r * tm:(r + 1) * tm, :], g_ref[...], mod)
            ht_ref[:, r * tm:(r + 1) * tm] = hh.T.astype(BF16)
        dt = _dot(wdt_ref[...], ht_ref[...]) + dtb_ref[...]
        dt_ref[0] = jnp.maximum(dt, 0.0) + jnp.log1p(jnp.exp(-jnp.abs(dt)))

    @pl.when(j < z_tiles)
    def _():
        z_ref[0] = _silu(_dot(w_ref[...], ht_ref[...])).astype(BF16)

    @pl.when(j >= z_tiles)
    def _():
        t = lax.broadcasted_iota(jnp.int32, (1, s), 1)
        has_prev = (t != 0) & (t != ctx_len)
        has_next = (t != ctx_len - 1) & (t != s - 1)
        half = w_ref.shape[0] // 2
        for r0 in (0, half):
            rs = slice(r0, r0 + half)
            u = _dot(w_ref[rs, :], ht_ref[...])
            up = jnp.where(has_prev, pltpu.roll(u, 1, 1), 0.0)
            un = jnp.where(has_next, pltpu.roll(u, s - 1, 1), 0.0)
            cw = cw_ref[rs, :]
            v = cw[:, 0:1] * up + cw[:, 1:2] * u + cw[:, 2:3] * un + cb_ref[rs, :]
            xbc_ref[0, rs, :] = _silu(v).astype(BF16)


def _proj_c(xc, mod, g, wt_bf, wdt_bf, dtb, cwt, cbt, ctx_len, bsz, inner):
    b, s, d = xc.shape
    tc = C_ROW_TILE
    nrows = wt_bf.shape[0]
    z_tiles = inner // tc
    n_tiles = nrows // tc
    conv_dim = nrows - inner
    ndt = wdt_bf.shape[0]
    body = functools.partial(_proj_c_body, ctx_len=ctx_len, z_tiles=z_tiles)
    const2 = lambda bb, j: (0, 0)
    return pl.pallas_call(
        body,
        out_shape=(jax.ShapeDtypeStruct((b, inner, s), BF16),
                   jax.ShapeDtypeStruct((b, conv_dim, s), BF16),
                   jax.ShapeDtypeStruct((b, ndt, s), F32)),
        grid=(b, n_tiles),
        in_specs=[pl.BlockSpec((1, s, d), lambda bb, j: (bb, 0, 0)),
                  pl.BlockSpec((1, 3, d), lambda bb, j: (bsz, 0, 0)),
                  pl.BlockSpec((1, 3, d), lambda bb, j: (bb, 0, 0)),
                  pl.BlockSpec((1, d), const2),
                  pl.BlockSpec((ndt, d), const2),
                  pl.BlockSpec((ndt, 1), const2),
                  pl.BlockSpec((tc, d), lambda bb, j: (j, 0)),
                  pl.BlockSpec((tc, 3), lambda bb, j: (jnp.maximum(j - z_tiles, 0), 0)),
                  pl.BlockSpec((tc, 1), lambda bb, j: (jnp.maximum(j - z_tiles, 0), 0))],
        out_specs=(pl.BlockSpec((1, tc, s), lambda bb, j: (bb, jnp.minimum(j, z_tiles - 1), 0)),
                   pl.BlockSpec((1, tc, s), lambda bb, j: (bb, jnp.maximum(j - z_tiles, 0), 0)),
                   pl.BlockSpec((1, ndt, s), lambda bb, j: (bb, 0, 0))),
        scratch_shapes=[pltpu.VMEM((d, s), BF16)],
        compiler_params=_cparams(("parallel", "arbitrary")),
        name="proj_c",
    )(xc, mod, mod, g, wdt_bf, dtb, wt_bf, cwt, cbt)


def _ssd_chunk(t, n_chunks, ctx_chunks):
    tb = t - n_chunks
    cb = jnp.where(tb < ctx_chunks, ctx_chunks - 1 - tb, n_chunks - 1 - (tb - ctx_chunks))
    return jnp.where(t < n_chunks, t, cb)


def _rep_rows(a, reps):
    r = a.shape[0]
    return jnp.broadcast_to(a[:, None, :], (r, reps, a.shape[1])).reshape(r * reps, a.shape[1])


def _ssd_body(z_ref, xbc_ref, dt_ref, dtn_ref, a_ref, dsk_ref, nw_ref, x_ref, wo_ref, mod_ref,
              o_ref, yf_ref, st_ref, yb_ref, pro_ref, *, n_chunks, ctx_chunks, heads, inner):
    t = pl.program_id(1)
    q = C_CHUNK
    hpg = heads // C_GROUPS
    gw = hpg * C_HEAD_DIM
    bwd = t >= n_chunks
    d = bwd.astype(jnp.int32)
    c = _ssd_chunk(t, n_chunks, ctx_chunks)
    ii = lax.broadcasted_iota(jnp.int32, (q, q), 0)
    jj = lax.broadcasted_iota(jnp.int32, (q, q), 1)

    def decay_terms(dt_blk_ref, step):
        rev = step >= n_chunks
        rows = pl.ds(pl.multiple_of(rev.astype(jnp.int32) * heads, heads), heads)
        dtd = dt_blk_ref[0, rows, :]
        a = dtd * a_ref[rows, :]
        tri = (ii <= jj).astype(BF16)
        a_hi, a_lo = _split2(a)
        a_lo2 = (a - a_hi.astype(F32) - a_lo.astype(F32)).astype(BF16)
        cum = _dot(a_hi, tri) + _dot(a_lo, tri) + _dot(a_lo2, tri)
        tot = jnp.broadcast_to(cum[:, q - 1:q], cum.shape)
        u = jnp.where(rev, tot - cum + a, cum)
        pro_ref[0] = dtd
        pro_ref[1] = u * LOG2E
        pro_ref[2] = jnp.exp(u)
        pro_ref[3] = dtd * jnp.exp(tot - u)
        pro_ref[4] = jnp.exp(tot)

    @pl.when(t == 0)
    def _():
        decay_terms(dt_ref, t)

    @pl.when((t == 0) | (t == n_chunks))
    def _():
        st_ref[...] = jnp.zeros_like(st_ref)

    dtd, u2, eu, dtdend, etot = (pro_ref[i] for i in range(5))
    decay_terms(dtn_ref, t + 1)
    sgn = 1 - 2 * d
    keep = (jj - ii) * sgn >= 0

    for g in range(C_GROUPS):
        r0 = g * gw
        hs = slice(g * hpg, (g + 1) * hpg)
        b_t = xbc_ref[0, inner + g * C_STATE: inner + (g + 1) * C_STATE, :]
        c_t = xbc_ref[0, inner + (C_GROUPS + g) * C_STATE: inner + (C_GROUPS + g + 1) * C_STATE, :]
        b_g = b_t.astype(F32).T.astype(BF16)
        cb_t = _dot(b_g, c_t)
        xs = xbc_ref[0, r0:r0 + gw, :].astype(F32)
        x_bf = (xs * _rep_rows(dtd[hs], C_HEAD_DIM)).astype(BF16)
        st = st_ref[r0:r0 + gw, :]
        y_off = _dot(st.astype(BF16), c_t) * _rep_rows(eu[hs], C_HEAD_DIM)
        st_ref[r0:r0 + gw, :] = (_rep_rows(etot[hs], C_HEAD_DIM) * st
                                 + _dot((xs * _rep_rows(dtdend[hs], C_HEAD_DIM)).astype(BF16), b_g))
        for hh in range(hpg):
            hd = g * hpg + hh
            r = jnp.broadcast_to(u2[hd:hd + 1, :], (q, q))
            e = jnp.where(keep, r - r.T, NEG)
            m_t = (jnp.exp2(e) * cb_t).astype(BF16)
            rr = slice(hh * C_HEAD_DIM, (hh + 1) * C_HEAD_DIM)
            yb_ref[r0 + hh * C_HEAD_DIM:r0 + (hh + 1) * C_HEAD_DIM, :] = _dot(x_bf[rr], m_t) + y_off[rr]

    @pl.when(jnp.logical_not(bwd))
    def _():
        yf_ref[c] = yb_ref[...] + dsk_ref[...] * xbc_ref[0, 0:inner, :].astype(F32)

    @pl.when(bwd)
    def _():
        acts = []
        for g in range(C_GROUPS):
            r0 = g * gw
            rs = slice(r0, r0 + gw)
            y = yf_ref[c, rs, :] + yb_ref[rs, :]
            y = y * z_ref[0, rs, :].astype(F32)
            ms = jnp.mean(y * y, axis=0, keepdims=True)
            y = y * lax.rsqrt(ms + EPS) * nw_ref[rs, :]
            acts.append(y.T.astype(BF16))
        o_ref[0] = x_ref[0] + mod_ref[0][2:3] * _dot(jnp.concatenate(acts, axis=1), wo_ref[...])


def _ssd(z_t, xbc_t, dt_t, a_tab, dsk_tab, nw_tab, xc, wo_bf, mod, ctx_len, bsz):
    b, inner, s = z_t.shape
    d = xc.shape[2]
    conv_dim = xbc_t.shape[1]
    heads = inner // C_HEAD_DIM
    n_chunks = s // C_CHUNK
    ctx_chunks = ctx_len // C_CHUNK
    body = functools.partial(_ssd_body, n_chunks=n_chunks, ctx_chunks=ctx_chunks, heads=heads,
                             inner=inner)
    cmap = lambda bb, t: (bb, 0, _ssd_chunk(t, n_chunks, ctx_chunks))
    nmap = lambda bb, t: (bb, 0, _ssd_chunk(jnp.minimum(t + 1, 2 * n_chunks - 1), n_chunks, ctx_chunks))
    out_chunk = lambda t: _ssd_chunk(jnp.maximum(t, n_chunks), n_chunks, ctx_chunks)
    omap = lambda bb, t: (bb, out_chunk(t), 0)
    modrow = lambda bb, t: (jnp.where(out_chunk(t) < ctx_chunks, bsz, bb), 0, 0)
    const2 = lambda bb, t: (0, 0)
    return pl.pallas_call(
        body,
        out_shape=jax.ShapeDtypeStruct(xc.shape, F32),
        grid=(b, 2 * n_chunks),
        in_specs=[pl.BlockSpec((1, inner, C_CHUNK), cmap),
                  pl.BlockSpec((1, conv_dim, C_CHUNK), cmap),
                  pl.BlockSpec((1, 2 * heads, C_CHUNK), cmap),
                  pl.BlockSpec((1, 2 * heads, C_CHUNK), nmap),
                  pl.BlockSpec((2 * heads, LANES), const2),
                  pl.BlockSpec((inner, LANES), const2),
                  pl.BlockSpec((inner, LANES), const2),
                  pl.BlockSpec((1, C_CHUNK, d), omap),
                  pl.BlockSpec(wo_bf.shape, const2),
                  pl.BlockSpec((1, 3, d), modrow)],
        out_specs=pl.BlockSpec((1, C_CHUNK, d), omap),
        scratch_shapes=[pltpu.VMEM((n_chunks, inner, C_CHUNK), F32),
                        pltpu.VMEM((inner, C_STATE), F32),
                        pltpu.VMEM((inner, C_CHUNK), F32),
                        pltpu.VMEM((5, heads, C_CHUNK), F32)],
        input_output_aliases={7: 0},
        compiler_params=_cparams(("parallel", "arbitrary")),
        name="ssd_scan",
    )(z_t, xbc_t, dt_t, dt_t, a_tab, dsk_tab, nw_tab, xc, wo_bf, mod)


def _lanes(v):
    return jnp.broadcast_to(v.astype(F32)[:, None], (v.shape[0], LANES))


def kernel(x, c, ctx, c_ctx, ada_w, ada_b, norm_g, final_g, a_w_in, a_sink, a_w_out, b_w_in, b_q_norm, b_w_uq, b_kv_norm, b_w_ukv, b_w_out, c_w_in, c_conv_w, c_conv_b, c_dt_bias, c_a_log, c_d, c_norm, c_w_out):
    bsz, t_len, d = x.shape
    ctx_len = ctx.shape[1]
    depth = ada_w.shape[0]
    assert ctx_len % ROW_TILE == 0 and t_len % ROW_TILE == 0 and t_len % GRID_W == 0
    n_ctx_tiles = ctx_len // ROW_TILE
    rows = t_len // GRID_W

    r_pad = -(-(bsz + 1) // 8) * 8
    cond = jnp.concatenate([c, c_ctx[None], jnp.zeros((r_pad - bsz - 1, d), F32)], axis=0)
    mods = _adaln_all(cond, ada_w, ada_b).reshape(depth, r_pad, 3, d)

    lane = jnp.arange(LANES)
    a_scale = A_HEAD_DIM ** -0.5 * LOG2E
    a_lane_dim = lane % A_HEAD_DIM
    a_tabs = (_rope_tables_t(rows, A_HEAD_DIM, ctx_len, a_scale)
              + _rope_tables(rows, A_HEAD_DIM, ctx_len, a_lane_dim, 1.0))
    b_scale = (B_NOPE + B_ROPE) ** -0.5 * LOG2E
    b_lane_dim = jnp.where((lane >= B_NOPE) & (lane < B_NOPE + B_ROPE), lane - B_NOPE, -1)
    b_tabs = (_rope_tables_t(rows, B_ROPE, ctx_len, b_scale)
              + _rope_tables(rows, B_ROPE, ctx_len, b_lane_dim, 1.0))

    for i in range(depth):
        kind = i % N_MIXERS
        j = i // N_MIXERS
        last = i == depth - 1
        mod = mods[i]
        g = norm_g[i][None]
        if kind == 0:
            w = a_w_in[j]
            qw, kw = A_HEADS * A_HEAD_DIM, A_KV_HEADS * A_HEAD_DIM
            w_tok = jnp.concatenate([w[:, qw:qw + kw], w[:, qw + 2 * kw:]], axis=1)
            xs = (ctx, x) if i == 0 else (xc,)
            q_t, k, v_t, sg = _proj_a(xs, mod, g, w[:, :qw].T.astype(BF16),
                                      w[:, qw + kw:qw + 2 * kw].T.astype(BF16), w_tok.astype(BF16),
                                      a_tabs, n_ctx_tiles, bsz)
            xc = _attn_a(q_t, k, v_t, sg, a_sink[j].astype(F32), xs, a_w_out[j].astype(BF16), mod,
                         ctx_len, bsz, final_g=final_g[None] if last else None)
        elif kind == 1:
            w = b_w_in[j]
            q_lora = b_q_norm.shape[1]
            kv_lora = b_kv_norm.shape[1]
            k0 = q_lora + kv_lora
            w_pad = jnp.concatenate(
                [w[:, :k0], jnp.zeros((d, B_NOPE), F32), w[:, k0:k0 + B_ROPE],
                 jnp.zeros((d, LANES - B_NOPE - B_ROPE), F32), w[:, k0 + B_ROPE:]], axis=1)
            wuq = b_w_uq[j].reshape(q_lora, B_HEADS, B_NOPE + B_ROPE)
            wuq = jnp.pad(wuq, ((0, 0), (0, 0), (0, LANES - B_NOPE - B_ROPE))).reshape(q_lora, B_HEADS * LANES)
            wukv = b_w_ukv[j].reshape(kv_lora, B_HEADS, B_NOPE + B_V)
            wkn = wukv[:, :, :B_NOPE].reshape(kv_lora, B_HEADS * B_NOPE)
            wv = wukv[:, :, B_NOPE:].reshape(kv_lora, B_HEADS * B_V)
            q_t, kp, v_t, sg = _proj_b(xc, mod, g, w_pad.astype(BF16), b_q_norm[j][None],
                                       wuq.T.astype(BF16), b_kv_norm[j][None], wkn.astype(BF16),
                                       wv.T.astype(BF16), b_tabs, b_scale, n_ctx_tiles, bsz)
            xc = _attn_b(q_t, kp, v_t, sg, xc, b_w_out[j].astype(BF16), mod, ctx_len, bsz)
        else:
            inner = c_norm.shape[1]
            heads = inner // C_HEAD_DIM
            conv_dim = c_conv_w.shape[2]
            w = c_w_in[j]
            wt = w[:, :inner + conv_dim].T.astype(BF16)
            wdt = w[:, inner + conv_dim:].T.astype(BF16)
            z_t, xbc_t, dt_t = _proj_c(xc, mod, g, wt, wdt, c_dt_bias[j].reshape(2 * heads, 1),
                                       c_conv_w[j].T, c_conv_b[j][:, None], ctx_len, bsz, inner)
            a_tab = _lanes(-jnp.exp(c_a_log[j].astype(F32)).reshape(2 * heads))
            dsk_tab = _lanes(jnp.repeat(c_d[j], C_HEAD_DIM))
            xc = _ssd(z_t, xbc_t, dt_t, a_tab, dsk_tab, _lanes(c_norm[j]), xc, c_w_out[j].astype(BF16), mod,
                      ctx_len, bsz)
    assert (depth - 1) % N_MIXERS == 0
    return xc
```
